```python
import math
import jax, jax.numpy as jnp
from jax import lax
import numpy as np

D_MODEL = 2048
BATCH = 2
SEQ = 8192
DEPTH = 4
DEC_BATCH = 2
DEC_SEQ = 4096
PAST_LEN = 128

N_MIXERS = 2
N_A_LAYERS = (DEPTH + 1) // 2
N_B_LAYERS = DEPTH // 2
BLK = 128
A_HEADS = 16
A_KV_HEADS = 4
A_GROUP = A_HEADS // A_KV_HEADS
A_HEAD_DIM = D_MODEL // A_HEADS
WINDOW = 128
B_HEADS = 16
Q_LORA = 448
KV_LORA = 128
NOPE_DIM = 128
ROPE_DIM = 64
V_DIM = 128
D_FF = 4 * D_MODEL
ROPE_THETA = 10000.0
LN_EPS = 1e-5
RMS_EPS = 1e-6
ALPHA = (2 * DEPTH) ** 0.25
BETA = (8 * DEPTH) ** -0.25

kernel_name = 'hybrid_swa_sink_mla_deepnorm_encoder'


def layer_norm(x, g, b):
    xf = x.astype(jnp.float32)
    mu = jnp.mean(xf, axis=-1, keepdims=True)
    var = jnp.mean(jnp.square(xf - mu), axis=-1, keepdims=True)
    y = (xf - mu) * lax.rsqrt(var + LN_EPS) * g.astype(jnp.float32) + b.astype(jnp.float32)
    return y.astype(x.dtype)


def rms_norm(x, g):
    xf = x.astype(jnp.float32)
    y = xf * lax.rsqrt(jnp.mean(jnp.square(xf), axis=-1, keepdims=True) + RMS_EPS) * g.astype(jnp.float32)
    return y.astype(x.dtype)


def rope_tables(seq, dim, dtype):
    inv = 1.0 / (ROPE_THETA ** (jnp.arange(0, dim, 2, dtype=jnp.float32) / dim))
    ang = jnp.arange(seq, dtype=jnp.float32)[:, None] * inv[None, :]
    ang = jnp.concatenate([ang, ang], axis=-1)
    return jnp.cos(ang).astype(dtype), jnp.sin(ang).astype(dtype)


def apply_rope(x, cos, sin):
    shape = (1, x.shape[1]) + (1,) * (x.ndim - 3) + (x.shape[-1],)
    c = cos.reshape(shape)
    s = sin.reshape(shape)
    half = x.shape[-1] // 2
    rot = jnp.concatenate([-x[..., half:], x[..., :half]], axis=-1)
    return x * c + rot * s


def window_gqa_sink(x, w_qkv, sink, w_o):
    B, S, _ = x.shape
    nb = S // BLK
    qkv = x @ w_qkv
    q = qkv[..., :A_HEADS * A_HEAD_DIM].reshape(B, S, A_KV_HEADS, A_GROUP, A_HEAD_DIM)
    k = qkv[..., A_HEADS * A_HEAD_DIM:(A_HEADS + A_KV_HEADS) * A_HEAD_DIM].reshape(B, S, A_KV_HEADS, A_HEAD_DIM)
    v = qkv[..., (A_HEADS + A_KV_HEADS) * A_HEAD_DIM:].reshape(B, S, A_KV_HEADS, A_HEAD_DIM)
    cos, sin = rope_tables(S, A_HEAD_DIM, x.dtype)
    q = apply_rope(q, cos, sin)
    k = apply_rope(k, cos, sin)
    pad = ((0, 0), (WINDOW, WINDOW), (0, 0), (0, 0))
    k_pad = jnp.pad(k, pad)
    v_pad = jnp.pad(v, pad)
    sink_l = sink.astype(jnp.float32).reshape(A_KV_HEADS, A_GROUP)[None, :, :, None, None]
    scale = 1.0 / math.sqrt(A_HEAD_DIM)
    span = BLK + 2 * WINDOW

    def block(n):
        q0 = n * BLK
        qb = lax.dynamic_slice_in_dim(q, q0, BLK, axis=1)
        kb = lax.dynamic_slice_in_dim(k_pad, q0, span, axis=1)
        vb = lax.dynamic_slice_in_dim(v_pad, q0, span, axis=1)
        s = jnp.einsum('bqhgd,bkhd->bhgqk', qb, kb).astype(jnp.float32) * scale
        qi = q0 + jnp.arange(BLK)
        kj = q0 - WINDOW + jnp.arange(span)
        mask = (jnp.abs(qi[:, None] - kj[None, :]) <= WINDOW) & (kj >= 0)[None, :] & (kj < S)[None, :]
        s = jnp.where(mask, s, -jnp.inf)
        m = jnp.maximum(jnp.max(s, axis=-1, keepdims=True), sink_l)
        p = jnp.exp(s - m)
        denom = jnp.sum(p, axis=-1, keepdims=True) + jnp.exp(sink_l - m)
        return jnp.einsum('bhgqk,bkhd->bqhgd', (p / denom).astype(x.dtype), vb)

    out = lax.map(block, jnp.arange(nb))
    out = jnp.transpose(out, (1, 0, 2, 3, 4, 5)).reshape(B, S, A_HEADS * A_HEAD_DIM)
    return out @ w_o


def mla(x, w_dqkv, q_norm_g, kv_norm_g, w_uq, w_ukv, w_o):
    B, S, _ = x.shape
    nb = S // BLK
    lat = x @ w_dqkv
    c_q = rms_norm(lat[..., :Q_LORA], q_norm_g)
    c_kv = rms_norm(lat[..., Q_LORA:Q_LORA + KV_LORA], kv_norm_g)
    k_r = lat[..., Q_LORA + KV_LORA:]
    q = (c_q @ w_uq).reshape(B, S, B_HEADS, NOPE_DIM + ROPE_DIM)
    q_n, q_r = q[..., :NOPE_DIM], q[..., NOPE_DIM:]
    kv = (c_kv @ w_ukv).reshape(B, S, B_HEADS, NOPE_DIM + V_DIM)
    k_n, v = kv[..., :NOPE_DIM], kv[..., NOPE_DIM:]
    cos, sin = rope_tables(S, ROPE_DIM, x.dtype)
    q_r = apply_rope(q_r, cos, sin)
    k_r = apply_rope(k_r, cos, sin)
    scale = 1.0 / math.sqrt(NOPE_DIM + ROPE_DIM)

    def block(n):
        q0 = n * BLK
        qn = lax.dynamic_slice_in_dim(q_n, q0, BLK, axis=1)
        qr = lax.dynamic_slice_in_dim(q_r, q0, BLK, axis=1)
        s = (jnp.einsum('bqhd,bkhd->bhqk', qn, k_n).astype(jnp.float32)
             + jnp.einsum('bqhr,bkr->bhqk', qr, k_r).astype(jnp.float32)) * scale
        p = jax.nn.softmax(s, axis=-1).astype(x.dtype)
        return jnp.einsum('bhqk,bkhd->bqhd', p, v)

    out = lax.map(block, jnp.arange(nb))
    out = jnp.transpose(out, (1, 0, 2, 3, 4)).reshape(B, S, B_HEADS * V_DIM)
    return out @ w_o


def sq_relu_mlp(x, w_up, w_down):
    return jnp.square(jax.nn.relu(x @ w_up)) @ w_down


def trunk(x, w_qkv_a, sink_a, w_o_a, w_dqkv_b, q_norm_b, kv_norm_b, w_uq_b, w_ukv_b, w_o_b,
          w_up, w_down, ln_g, ln_b):
    for i in range(DEPTH):
        j = i // N_MIXERS
        if i % N_MIXERS == 0:
            h = window_gqa_sink(x, w_qkv_a[j], sink_a[j], w_o_a[j])
        else:
            h = mla(x, w_dqkv_b[j], q_norm_b[j], kv_norm_b[j], w_uq_b[j], w_ukv_b[j], w_o_b[j])
        x = layer_norm(ALPHA * x + h, ln_g[i, 0], ln_b[i, 0])
        x = layer_norm(ALPHA * x + sq_relu_mlp(x, w_up[i], w_down[i]), ln_g[i, 1], ln_b[i, 1])
    return x


def _normal(key, shape, scale):
    return jax.random.normal(key, shape, jnp.float32) * scale


def setup_inputs(seed: int = 0) -> dict:
    key = jax.random.key(seed)
    ks = jax.random.split(key, 16)
    qkv_a_cols = (A_HEADS + 2 * A_KV_HEADS) * A_HEAD_DIM
    return {
        'x_prompt': _normal(ks[0], (BATCH, SEQ, D_MODEL), 1.0),
        'x_sample': _normal(ks[1], (DEC_BATCH, DEC_SEQ, D_MODEL), 1.0),
        'w_qkv_a': _normal(ks[2], (N_A_LAYERS, D_MODEL, qkv_a_cols), D_MODEL ** -0.5),
        'sink_a': _normal(ks[3], (N_A_LAYERS, A_HEADS), 0.5),
        'w_o_a': _normal(ks[4], (N_A_LAYERS, A_HEADS * A_HEAD_DIM, D_MODEL), BETA * (A_HEADS * A_HEAD_DIM) ** -0.5),
        'w_dqkv_b': _normal(ks[5], (N_B_LAYERS, D_MODEL, Q_LORA + KV_LORA + ROPE_DIM), D_MODEL ** -0.5),
        'q_norm_b': 1.0 + _normal(ks[6], (N_B_LAYERS, Q_LORA), 0.02),
        'kv_norm_b': 1.0 + _normal(ks[7], (N_B_LAYERS, KV_LORA), 0.02),
        'w_uq_b': _normal(ks[8], (N_B_LAYERS, Q_LORA, B_HEADS * (NOPE_DIM + ROPE_DIM)), Q_LORA ** -0.5),
        'w_ukv_b': _normal(ks[9], (N_B_LAYERS, KV_LORA, B_HEADS * (NOPE_DIM + V_DIM)), KV_LORA ** -0.5),
        'w_o_b': _normal(ks[10], (N_B_LAYERS, B_HEADS * V_DIM, D_MODEL), BETA * (B_HEADS * V_DIM) ** -0.5),
        'w_up': _normal(ks[11], (DEPTH, D_MODEL, D_FF), D_MODEL ** -0.5),
        'w_down': _normal(ks[12], (DEPTH, D_FF, D_MODEL), BETA * D_FF ** -0.5),
        'ln_g': 1.0 + _normal(ks[13], (DEPTH, 2, D_MODEL), 0.02),
        'ln_b': _normal(ks[14], (DEPTH, 2, D_MODEL), 0.02),
    }


def reference(x_prompt, x_sample, w_qkv_a, sink_a, w_o_a, w_dqkv_b, q_norm_b, kv_norm_b,
              w_uq_b, w_ukv_b, w_o_b, w_up, w_down, ln_g, ln_b):
    y_prompt = trunk(x_prompt, w_qkv_a, sink_a, w_o_a, w_dqkv_b, q_norm_b, kv_norm_b,
                     w_uq_b, w_ukv_b, w_o_b, w_up, w_down, ln_g, ln_b)
    y_sample = trunk(x_sample, w_qkv_a, sink_a, w_o_a, w_dqkv_b, q_norm_b, kv_norm_b,
                     w_uq_b, w_ukv_b, w_o_b, w_up, w_down, ln_g, ln_b)
    return (y_prompt, y_sample)
```

```python
import functools
import math

import jax
import jax.numpy as jnp
from jax import lax
from jax.experimental import pallas as pl
from jax.experimental.pallas import tpu as pltpu

D_MODEL = 2048
DEPTH = 4
N_MIXERS = 2
BLK = 128
A_HEADS = 16
A_KV_HEADS = 4
A_GROUP = A_HEADS // A_KV_HEADS
A_HEAD_DIM = D_MODEL // A_HEADS
WINDOW = 128
B_HEADS = 16
Q_LORA = 448
KV_LORA = 128
NOPE_DIM = 128
ROPE_DIM = 64
V_DIM = 128
D_FF = 4 * D_MODEL
ROPE_THETA = 10000.0
LN_EPS = 1e-5
RMS_EPS = 1e-6
ALPHA = (2 * DEPTH) ** 0.25

LANES = 128
MXU_DIM = 256
VMEM_LIMIT_BYTES = 56 * 1024 * 1024

Q_LORA_PAD = 512
LAT_COLS = Q_LORA_PAD + KV_LORA + 2 * ROPE_DIM
QK_SLAB = MXU_DIM

F32 = jnp.float32
BF16 = jnp.bfloat16


def _compiler_params(semantics):
    return pltpu.CompilerParams(dimension_semantics=semantics,
                                vmem_limit_bytes=VMEM_LIMIT_BYTES)


def _layer_norm_rows(y, g, b):
    mu = jnp.mean(y, axis=-1, keepdims=True)
    d = y - mu
    var = jnp.mean(d * d, axis=-1, keepdims=True)
    return d * lax.rsqrt(var + LN_EPS) * g + b


def _dot_nt(a, b):
    return lax.dot_general(a, b, (((1,), (1,)), ((), ())), preferred_element_type=F32)


def _qkv_rope_kernel(x_ref, w_ref, cos_ref, sin_ref, q_ref, k_ref, v_ref):
    xb = x_ref[...].astype(BF16)
    cos = cos_ref[...]
    sin = sin_ref[...]
    chunk = A_GROUP * A_HEAD_DIM
    q_cols = A_HEADS * A_HEAD_DIM
    k_cols = A_KV_HEADS * A_HEAD_DIM

    def rope(t):
        return t * cos + pltpu.roll(t, A_HEAD_DIM // 2, 1) * sin

    for c in range(0, q_cols + k_cols, chunk):
        acc = jnp.dot(xb, w_ref[:, c:c + chunk], preferred_element_type=F32)
        for hh in range(chunk // A_HEAD_DIM):
            y = rope(acc[:, hh * A_HEAD_DIM:(hh + 1) * A_HEAD_DIM]).astype(BF16)
            col = c + hh * A_HEAD_DIM
            if col < q_cols:
                q_ref[:, col:col + A_HEAD_DIM] = y
            else:
                k_ref[:, col - q_cols:col - q_cols + A_HEAD_DIM] = y
    v0 = q_cols + k_cols
    v_ref[...] = jnp.dot(xb, w_ref[:, v0:v0 + k_cols],
                         preferred_element_type=F32).astype(BF16)


def _qkv_rope(x, w, cos, sin, seq, tm):
    t = x.shape[0]
    tiles_per_seq = seq // tm
    q_cols = A_HEADS * A_HEAD_DIM
    k_cols = A_KV_HEADS * A_HEAD_DIM
    row = lambda i: (i, 0)
    pos = lambda i: (i % tiles_per_seq, 0)
    return pl.pallas_call(
        _qkv_rope_kernel,
        grid=(t // tm,),
        in_specs=[
            pl.BlockSpec((tm, D_MODEL), row),
            pl.BlockSpec((D_MODEL, q_cols + 2 * k_cols), lambda i: (0, 0)),
            pl.BlockSpec((tm, A_HEAD_DIM), pos),
            pl.BlockSpec((tm, A_HEAD_DIM), pos),
        ],
        out_specs=[
            pl.BlockSpec((tm, q_cols), row),
            pl.BlockSpec((tm, k_cols), row),
            pl.BlockSpec((tm, k_cols), row),
        ],
        out_shape=[
            jax.ShapeDtypeStruct((t, q_cols), BF16),
            jax.ShapeDtypeStruct((t, k_cols), BF16),
            jax.ShapeDtypeStruct((t, k_cols), BF16),
        ],
        compiler_params=_compiler_params(("parallel",)),
        name="qkv_rope",
    )(x, w, cos, sin)


def _win_attn_kernel(sink_ref, q_ref, kp_ref, kc_ref, kn_ref, vp_ref, vc_ref, vn_ref,
                     o_ref, *, nb):
    n = pl.program_id(1)
    rows = A_GROUP * BLK
    span = BLK + 2 * WINDOW
    qi = lax.broadcasted_iota(jnp.int32, (rows, span), 0) & (BLK - 1)
    col = lax.broadcasted_iota(jnp.int32, (rows, span), 1)
    lo = jnp.where(n > 0, 0, WINDOW)
    hi = jnp.where(n < nb - 1, span, BLK + WINDOW)
    valid = (col >= jnp.maximum(qi, lo)) & (col <= qi + 2 * WINDOW) & (col < hi)
    bias = jnp.where(valid, 0.0, -jnp.inf).astype(F32)
    grp = lax.broadcasted_iota(jnp.int32, (rows, 1), 0) // BLK
    scale = 1.0 / math.sqrt(A_HEAD_DIM)

    for h in range(A_KV_HEADS):
        ks = slice(h * A_HEAD_DIM, (h + 1) * A_HEAD_DIM)
        qh = jnp.concatenate(
            [q_ref[:, (h * A_GROUP + g) * A_HEAD_DIM:(h * A_GROUP + g + 1) * A_HEAD_DIM]
             for g in range(A_GROUP)], axis=0)
        kh = jnp.concatenate([kp_ref[:, ks], kc_ref[:, ks], kn_ref[:, ks]], axis=0)
        vh = jnp.concatenate([vp_ref[:, ks], vc_ref[:, ks], vn_ref[:, ks]], axis=0)
        s = _dot_nt(qh, kh) * scale + bias
        sink = jnp.full((rows, 1), sink_ref[h * A_GROUP], F32)
        for g in range(1, A_GROUP):
            sink = jnp.where(grp == g, sink_ref[h * A_GROUP + g], sink)
        m = jnp.maximum(jnp.max(s, axis=-1, keepdims=True), sink)
        p = jnp.exp(s - m)
        denom = jnp.sum(p, axis=-1, keepdims=True) + jnp.exp(sink - m)
        o = jnp.dot((p / denom).astype(BF16), vh, preferred_element_type=F32)
        for g in range(A_GROUP):
            c0 = (h * A_GROUP + g) * A_HEAD_DIM
            o_ref[:, c0:c0 + A_HEAD_DIM] = o[g * BLK:(g + 1) * BLK].astype(BF16)


def _win_attn(q, k, v, sink, batch, seq):
    t = q.shape[0]
    nb = seq // BLK
    q_cols = A_HEADS * A_HEAD_DIM
    k_cols = A_KV_HEADS * A_HEAD_DIM
    cur = lambda b, n: (b * nb + n, 0)
    prev = lambda b, n: (b * nb + jnp.maximum(n - 1, 0), 0)
    nxt = lambda b, n: (b * nb + jnp.minimum(n + 1, nb - 1), 0)
    kv_spec = lambda im: pl.BlockSpec((BLK, k_cols), im)
    return pl.pallas_call(
        functools.partial(_win_attn_kernel, nb=nb),
        grid=(batch, nb),
        in_specs=[
            pl.BlockSpec(memory_space=pltpu.SMEM),
            pl.BlockSpec((BLK, q_cols), cur),
            kv_spec(prev), kv_spec(cur), kv_spec(nxt),
            kv_spec(prev), kv_spec(cur), kv_spec(nxt),
        ],
        out_specs=pl.BlockSpec((BLK, q_cols), cur),
        out_shape=jax.ShapeDtypeStruct((t, q_cols), BF16),
        compiler_params=_compiler_params(("parallel", "parallel")),
        name="win_attn",
    )(sink, q, k, k, k, v, v, v)


def _proj_res_ln_kernel(a_ref, w_ref, x_ref, g_ref, b_ref, o_ref):
    h = jnp.dot(a_ref[...], w_ref[...], preferred_element_type=F32)
    o_ref[...] = _layer_norm_rows(ALPHA * x_ref[...] + h, g_ref[...], b_ref[...])


def _proj_res_ln(a, w, x, g, b, tm):
    t, kdim = a.shape
    row = lambda i: (i, 0)
    const = lambda i: (0, 0)
    return pl.pallas_call(
        _proj_res_ln_kernel,
        grid=(t // tm,),
        in_specs=[
            pl.BlockSpec((tm, kdim), row),
            pl.BlockSpec((kdim, D_MODEL), const),
            pl.BlockSpec((tm, D_MODEL), row),
            pl.BlockSpec((1, D_MODEL), const),
            pl.BlockSpec((1, D_MODEL), const),
        ],
        out_specs=pl.BlockSpec((tm, D_MODEL), row),
        out_shape=jax.ShapeDtypeStruct((t, D_MODEL), F32),
        compiler_params=_compiler_params(("parallel",)),
        name="proj_res_ln",
    )(a, w, x, g, b)


def _mlp_kernel(x_ref, wu_ref, wd_ref, g_ref, b_ref, o_ref, xb_ref):
    j = pl.program_id(1)

    @pl.when(j == 0)
    def _():
        xb_ref[...] = x_ref[...].astype(BF16)
        o_ref[...] = jnp.zeros_like(o_ref)

    h = jnp.dot(xb_ref[...], wu_ref[...], preferred_element_type=F32)
    h = jnp.square(jnp.maximum(h, 0.0)).astype(BF16)
    o_ref[...] += jnp.dot(h, wd_ref[...], preferred_element_type=F32)

    @pl.when(j == pl.num_programs(1) - 1)
    def _():
        o_ref[...] = _layer_norm_rows(ALPHA * x_ref[...] + o_ref[...], g_ref[...], b_ref[...])


def _mlp(x, wu, wd, g, b, tm, tf):
    t = x.shape[0]
    row = lambda i, j: (i, 0)
    const = lambda i, j: (0, 0)
    return pl.pallas_call(
        _mlp_kernel,
        grid=(t // tm, D_FF // tf),
        in_specs=[
            pl.BlockSpec((tm, D_MODEL), row),
            pl.BlockSpec((D_MODEL, tf), lambda i, j: (0, j)),
            pl.BlockSpec((tf, D_MODEL), lambda i, j: (j, 0)),
            pl.BlockSpec((1, D_MODEL), const),
            pl.BlockSpec((1, D_MODEL), const),
        ],
        out_specs=pl.BlockSpec((tm, D_MODEL), row),
        out_shape=jax.ShapeDtypeStruct((t, D_MODEL), F32),
        scratch_shapes=[pltpu.VMEM((tm, D_MODEL), BF16)],
        compiler_params=_compiler_params(("parallel", "arbitrary")),
        name="mlp",
    )(x, wu, wd, g, b)


def _mla_proj_kernel(x_ref, wd_ref, gq_ref, gkv_ref, wuq_ref, wukv_ref, tab_ref,
                     q_ref, k_ref, v_ref):
    xb = x_ref[...].astype(BF16)
    tab = tab_ref[...]
    lane = lax.broadcasted_iota(jnp.int32, tab.shape, 1)

    def rope_pair(y):
        t = y * tab
        return jnp.where(lane < ROPE_DIM, t + pltpu.roll(t, ROPE_DIM, 1), 0.0).astype(BF16)

    lat = jnp.dot(xb, wd_ref[...], preferred_element_type=F32)
    cq = lat[:, :Q_LORA_PAD]
    cq = cq * lax.rsqrt(jnp.sum(cq * cq, axis=-1, keepdims=True) * (1.0 / Q_LORA) + RMS_EPS)
    cq = (cq * gq_ref[...]).astype(BF16)
    ckv = lat[:, Q_LORA_PAD:Q_LORA_PAD + KV_LORA]
    ckv = ckv * lax.rsqrt(jnp.mean(ckv * ckv, axis=-1, keepdims=True) + RMS_EPS)
    ckv = (ckv * gkv_ref[...]).astype(BF16)
    kr = rope_pair(lat[:, Q_LORA_PAD + KV_LORA:])

    heads_per_chunk = 2
    qchunk = heads_per_chunk * QK_SLAB
    for c in range(0, B_HEADS * QK_SLAB, qchunk):
        qq = jnp.dot(cq, wuq_ref[:, c:c + qchunk], preferred_element_type=F32)
        for hh in range(heads_per_chunk):
            o = hh * QK_SLAB
            q_ref[:, c + o:c + o + NOPE_DIM] = qq[:, o:o + NOPE_DIM].astype(BF16)
            q_ref[:, c + o + NOPE_DIM:c + o + QK_SLAB] = rope_pair(qq[:, o + NOPE_DIM:o + QK_SLAB])

    kchunk = 4 * NOPE_DIM
    for c in range(0, B_HEADS * NOPE_DIM, kchunk):
        kn = jnp.dot(ckv, wukv_ref[:, c:c + kchunk], preferred_element_type=F32)
        for hh in range(kchunk // NOPE_DIM):
            h = c // NOPE_DIM + hh
            k_ref[:, h * QK_SLAB:h * QK_SLAB + NOPE_DIM] = (
                kn[:, hh * NOPE_DIM:(hh + 1) * NOPE_DIM].astype(BF16))
            k_ref[:, h * QK_SLAB + NOPE_DIM:(h + 1) * QK_SLAB] = kr
    v0 = B_HEADS * NOPE_DIM
    for c in range(0, B_HEADS * V_DIM, kchunk):
        v_ref[:, c:c + kchunk] = jnp.dot(ckv, wukv_ref[:, v0 + c:v0 + c + kchunk],
                                         preferred_element_type=F32).astype(BF16)


def _mla_proj(x, wd, gq, gkv, wuq, wukv, tab, seq, tm):
    t = x.shape[0]
    tiles_per_seq = seq // tm
    row = lambda i: (i, 0)
    const = lambda i: (0, 0)
    full = lambda a: pl.BlockSpec(a.shape, const)
    return pl.pallas_call(
        _mla_proj_kernel,
        grid=(t // tm,),
        in_specs=[
            pl.BlockSpec((tm, D_MODEL), row),
            full(wd), full(gq), full(gkv), full(wuq), full(wukv),
            pl.BlockSpec((tm, LANES), lambda i: (i % tiles_per_seq, 0)),
        ],
        out_specs=[
            pl.BlockSpec((tm, B_HEADS * QK_SLAB), row),
            pl.BlockSpec((tm, B_HEADS * QK_SLAB), row),
            pl.BlockSpec((tm, B_HEADS * V_DIM), row),
        ],
        out_shape=[
            jax.ShapeDtypeStruct((t, B_HEADS * QK_SLAB), BF16),
            jax.ShapeDtypeStruct((t, B_HEADS * QK_SLAB), BF16),
            jax.ShapeDtypeStruct((t, B_HEADS * V_DIM), BF16),
        ],
        compiler_params=_compiler_params(("parallel",)),
        name="mla_proj",
    )(x, wd, gq, gkv, wuq, wukv, tab)


def _mla_attn_kernel(q_ref, k_ref, v_ref, o_ref, m_ref, l_ref, acc_ref, *, tk):
    q = q_ref[...]
    scale = 1.0 / math.sqrt(NOPE_DIM + ROPE_DIM)
    m_ref[...] = jnp.full_like(m_ref, -jnp.inf)
    l_ref[...] = jnp.zeros_like(l_ref)
    acc_ref[...] = jnp.zeros_like(acc_ref)

    def body(c, carry):
        start = pl.multiple_of(c * tk, tk)
        s = _dot_nt(q, k_ref[pl.ds(start, tk), :]) * scale
        m_prev = m_ref[...]
        m_new = jnp.maximum(m_prev, jnp.max(s, axis=-1, keepdims=True))
        alpha = jnp.exp(m_prev - m_new)
        p = jnp.exp(s - m_new)
        l_ref[...] = alpha * l_ref[...] + jnp.sum(p, axis=-1, keepdims=True)
        acc_ref[...] = alpha * acc_ref[...] + jnp.dot(
            p.astype(BF16), v_ref[pl.ds(start, tk), :], preferred_element_type=F32)
        m_ref[...] = m_new
        return carry

    lax.fori_loop(0, k_ref.shape[0] // tk, body, 0)
    o_ref[...] = (acc_ref[...] / l_ref[...]).astype(BF16)


def _mla_attn(q, k, v, batch, seq, tq, tk):
    t = q.shape[0]
    nq = seq // tq
    return pl.pallas_call(
        functools.partial(_mla_attn_kernel, tk=tk),
        grid=(batch, B_HEADS, nq),
        in_specs=[
            pl.BlockSpec((tq, QK_SLAB), lambda b, h, i: (b * nq + i, h)),
            pl.BlockSpec((seq, QK_SLAB), lambda b, h, i: (b, h)),
            pl.BlockSpec((seq, V_DIM), lambda b, h, i: (b, h)),
        ],
        out_specs=pl.BlockSpec((tq, V_DIM), lambda b, h, i: (b * nq + i, h)),
        out_shape=jax.ShapeDtypeStruct((t, B_HEADS * V_DIM), BF16),
        scratch_shapes=[
            pltpu.VMEM((tq, 1), F32),
            pltpu.VMEM((tq, 1), F32),
            pltpu.VMEM((tq, V_DIM), F32),
        ],
        compiler_params=_compiler_params(("parallel", "parallel", "arbitrary")),
        name="mla_attn",
    )(q, k, v)


def _rope_angles(seq, dim):
    inv = 1.0 / (ROPE_THETA ** (jnp.arange(0, dim, 2, dtype=F32) / dim))
    ang = jnp.arange(seq, dtype=F32)[:, None] * inv[None, :]
    return jnp.concatenate([ang, ang], axis=-1)


def _rot_half_cols(w):
    half = w.shape[-1] // 2
    return jnp.concatenate([-w[:, half:], w[:, :half]], axis=-1)


def _prep_mla_weights(w_dqkv, q_norm, kv_norm, w_uq, w_ukv):
    wq = w_dqkv[:, :Q_LORA]
    wkv = w_dqkv[:, Q_LORA:Q_LORA + KV_LORA]
    wr = w_dqkv[:, Q_LORA + KV_LORA:]
    zpad = jnp.zeros((D_MODEL, Q_LORA_PAD - Q_LORA), w_dqkv.dtype)
    wd = jnp.concatenate([wq, zpad, wkv, wr, _rot_half_cols(wr)], axis=1).astype(BF16)
    gq = jnp.pad(q_norm, (0, Q_LORA_PAD - Q_LORA)).reshape(1, Q_LORA_PAD)
    gkv = kv_norm.reshape(1, KV_LORA)
    uq = w_uq.reshape(Q_LORA, B_HEADS, NOPE_DIM + ROPE_DIM)
    uq_r = uq[:, :, NOPE_DIM:]
    uq_rot = jnp.concatenate([-uq_r[..., ROPE_DIM // 2:], uq_r[..., :ROPE_DIM // 2]], axis=-1)
    wuq = jnp.concatenate([uq, uq_rot], axis=-1).reshape(Q_LORA, B_HEADS * QK_SLAB)
    wuq = jnp.pad(wuq, ((0, Q_LORA_PAD - Q_LORA), (0, 0))).astype(BF16)
    ukv = w_ukv.reshape(KV_LORA, B_HEADS, NOPE_DIM + V_DIM)
    wukv = jnp.concatenate([ukv[:, :, :NOPE_DIM].reshape(KV_LORA, B_HEADS * NOPE_DIM),
                            ukv[:, :, NOPE_DIM:].reshape(KV_LORA, B_HEADS * V_DIM)],
                           axis=1).astype(BF16)
    return wd, gq, gkv, wuq, wukv


def _tiles(seq):
    tm = 512
    assert seq % tm == 0 and seq % BLK == 0
    return dict(tm=tm, tf=1024, tq=512, tk=512)


def _trunk(x3, weights):
    batch, seq, _ = x3.shape
    cfg = _tiles(seq)
    tm = cfg["tm"]
    x = x3.reshape(batch * seq, D_MODEL)

    ang_a = _rope_angles(seq, A_HEAD_DIM)
    sign = jnp.where(jnp.arange(A_HEAD_DIM) < A_HEAD_DIM // 2, -1.0, 1.0).astype(F32)
    cos_a, sin_a = jnp.cos(ang_a), jnp.sin(ang_a) * sign
    ang_b = _rope_angles(seq, ROPE_DIM)
    tab_b = jnp.concatenate([jnp.cos(ang_b), jnp.sin(ang_b)], axis=-1)

    for i in range(DEPTH):
        j = i // N_MIXERS
        if i % N_MIXERS == 0:
            q, k, v = _qkv_rope(x, weights["w_qkv_a"][j], cos_a, sin_a, seq, tm)
            attn = _win_attn(q, k, v, weights["sink_a"][j], batch, seq)
            w_o = weights["w_o_a"][j]
        else:
            q, k, v = _mla_proj(x, *weights["mla"][j], tab_b, seq, tm)
            attn = _mla_attn(q, k, v, batch, seq, cfg["tq"], cfg["tk"])
            w_o = weights["w_o_b"][j]
        g, b = weights["ln_g"], weights["ln_b"]
        x = _proj_res_ln(attn, w_o, x, g[i, 0:1], b[i, 0:1], tm)
        x = _mlp(x, weights["w_up"][i], weights["w_down"][i], g[i, 1:2], b[i, 1:2],
                 tm, cfg["tf"])
    return x.reshape(batch, seq, D_MODEL)


def kernel(x_prompt, x_sample, w_qkv_a, sink_a, w_o_a, w_dqkv_b, q_norm_b, kv_norm_b,
           w_uq_b, w_ukv_b, w_o_b, w_up, w_down, ln_g, ln_b):
    weights = dict(
        w_qkv_a=w_qkv_a.astype(BF16),
        sink_a=sink_a.astype(F32),
        w_o_a=w_o_a.astype(BF16),
        mla=[_prep_mla_weights(w_dqkv_b[j], q_norm_b[j], kv_norm_b[j], w_uq_b[j], w_ukv_b[j])
             for j in range(w_dqkv_b.shape[0])],
        w_o_b=w_o_b.astype(BF16),
        w_up=w_up.astype(BF16),
        w_down=w_down.astype(BF16),
        ln_g=ln_g.astype(F32),
        ln_b=ln_b.astype(F32),
    )
    return (_trunk(x_prompt, weights), _trunk(x_sample, weights))
```

```python
import functools
import math

import jax
import jax.numpy as jnp
from jax import lax
from jax.experimental import pallas as pl
from jax.experimental.pallas import tpu as pltpu

D_MODEL = 2048
DEPTH = 4
N_MIXERS = 2
BLK = 128
A_HEADS = 16
A_KV_HEADS = 4
A_GROUP = A_HEADS // A_KV_HEADS
A_HEAD_DIM = D_MODEL // A_HEADS
WINDOW = 128
B_HEADS = 16
Q_LORA = 448
KV_LORA = 128
NOPE_DIM = 128
ROPE_DIM = 64
V_DIM = 128
D_FF = 4 * D_MODEL
ROPE_THETA = 10000.0
LN_EPS = 1e-5
RMS_EPS = 1e-6
ALPHA = (2 * DEPTH) ** 0.25

LANES = 128
MXU_DIM = 256
VMEM_LIMIT_BYTES = 56 * 1024 * 1024

Q_LORA_PAD = 512
LAT_COLS = Q_LORA_PAD + KV_LORA + 2 * ROPE_DIM
QK_SLAB = MXU_DIM

F32 = jnp.float32
BF16 = jnp.bfloat16


def _compiler_params(semantics):
    return pltpu.CompilerParams(dimension_semantics=semantics,
                                vmem_limit_bytes=VMEM_LIMIT_BYTES)


def _layer_norm_rows(y, g, b):
    mu = jnp.mean(y, axis=-1, keepdims=True)
    d = y - mu
    var = jnp.mean(d * d, axis=-1, keepdims=True)
    return d * lax.rsqrt(var + LN_EPS) * g + b


def _dot_nt(a, b):
    return lax.dot_general(a, b, (((1,), (1,)), ((), ())), preferred_element_type=F32)


def _qkv_rope_kernel(x_ref, w_ref, cos_ref, sin_ref, q_ref, k_ref, v_ref):
    xb = x_ref[...].astype(BF16)
    cos = cos_ref[...]
    sin = sin_ref[...]
    chunk = A_GROUP * A_HEAD_DIM
    q_cols = A_HEADS * A_HEAD_DIM
    k_cols = A_KV_HEADS * A_HEAD_DIM

    def rope(t):
        return t * cos + pltpu.roll(t, A_HEAD_DIM // 2, 1) * sin

    for c in range(0, q_cols + k_cols, chunk):
        acc = jnp.dot(xb, w_ref[:, c:c + chunk], preferred_element_type=F32)
        for hh in range(chunk // A_HEAD_DIM):
            y = rope(acc[:, hh * A_HEAD_DIM:(hh + 1) * A_HEAD_DIM]).astype(BF16)
            col = c + hh * A_HEAD_DIM
            if col < q_cols:
                q_ref[:, col:col + A_HEAD_DIM] = y
            else:
                k_ref[:, col - q_cols:col - q_cols + A_HEAD_DIM] = y
    v0 = q_cols + k_cols
    v_ref[...] = jnp.dot(xb, w_ref[:, v0:v0 + k_cols],
                         preferred_element_type=F32).astype(BF16)


def _qkv_rope(x, w, cos, sin, seq, tm):
    t = x.shape[0]
    tiles_per_seq = seq // tm
    q_cols = A_HEADS * A_HEAD_DIM
    k_cols = A_KV_HEADS * A_HEAD_DIM
    row = lambda i: (i, 0)
    pos = lambda i: (i % tiles_per_seq, 0)
    return pl.pallas_call(
        _qkv_rope_kernel,
        grid=(t // tm,),
        in_specs=[
            pl.BlockSpec((tm, D_MODEL), row),
            pl.BlockSpec((D_MODEL, q_cols + 2 * k_cols), lambda i: (0, 0)),
            pl.BlockSpec((tm, A_HEAD_DIM), pos),
            pl.BlockSpec((tm, A_HEAD_DIM), pos),
        ],
        out_specs=[
            pl.BlockSpec((tm, q_cols), row),
            pl.BlockSpec((tm, k_cols), row),
            pl.BlockSpec((tm, k_cols), row),
        ],
        out_shape=[
            jax.ShapeDtypeStruct((t, q_cols), BF16),
            jax.ShapeDtypeStruct((t, k_cols), BF16),
            jax.ShapeDtypeStruct((t, k_cols), BF16),
        ],
        compiler_params=_compiler_params(("parallel",)),
        name="qkv_rope",
    )(x, w, cos, sin)


def _win_attn_kernel(sink_ref, q_ref, kp_ref, kc_ref, kn_ref, vp_ref, vc_ref, vn_ref,
                     o_ref, *, nb):
    n = pl.program_id(1)
    rows = A_GROUP * BLK
    span = BLK + 2 * WINDOW
    qi = lax.broadcasted_iota(jnp.int32, (rows, span), 0) & (BLK - 1)
    col = lax.broadcasted_iota(jnp.int32, (rows, span), 1)
    lo = jnp.where(n > 0, 0, WINDOW)
    hi = jnp.where(n < nb - 1, span, BLK + WINDOW)
    valid = (col >= jnp.maximum(qi, lo)) & (col <= qi + 2 * WINDOW) & (col < hi)
    bias = jnp.where(valid, 0.0, -jnp.inf).astype(F32)
    grp = lax.broadcasted_iota(jnp.int32, (rows, 1), 0) // BLK
    scale = 1.0 / math.sqrt(A_HEAD_DIM)

    for h in range(A_KV_HEADS):
        ks = slice(h * A_HEAD_DIM, (h + 1) * A_HEAD_DIM)
        qh = jnp.concatenate(
            [q_ref[:, (h * A_GROUP + g) * A_HEAD_DIM:(h * A_GROUP + g + 1) * A_HEAD_DIM]
             for g in range(A_GROUP)], axis=0)
        kh = jnp.concatenate([kp_ref[:, ks], kc_ref[:, ks], kn_ref[:, ks]], axis=0)
        vh = jnp.concatenate([vp_ref[:, ks], vc_ref[:, ks], vn_ref[:, ks]], axis=0)
        s = _dot_nt(qh, kh) * scale + bias
        sink = jnp.full((rows, 1), sink_ref[h * A_GROUP], F32)
        for g in range(1, A_GROUP):
            sink = jnp.where(grp == g, sink_ref[h * A_GROUP + g], sink)
        m = jnp.maximum(jnp.max(s, axis=-1, keepdims=True), sink)
        p = jnp.exp(s - m)
        denom = jnp.sum(p, axis=-1, keepdims=True) + jnp.exp(sink - m)
        o = jnp.dot((p / denom).astype(BF16), vh, preferred_element_type=F32)
        for g in range(A_GROUP):
            c0 = (h * A_GROUP + g) * A_HEAD_DIM
            o_ref[:, c0:c0 + A_HEAD_DIM] = o[g * BLK:(g + 1) * BLK].astype(BF16)


def _win_attn(q, k, v, sink, batch, seq):
    t = q.shape[0]
    nb = seq // BLK
    q_cols = A_HEADS * A_HEAD_DIM
    k_cols = A_KV_HEADS * A_HEAD_DIM
    cur = lambda b, n: (b * nb + n, 0)
    prev = lambda b, n: (b * nb + jnp.maximum(n - 1, 0), 0)
    nxt = lambda b, n: (b * nb + jnp.minimum(n + 1, nb - 1), 0)
    kv_spec = lambda im: pl.BlockSpec((BLK, k_cols), im)
    return pl.pallas_call(
        functools.partial(_win_attn_kernel, nb=nb),
        grid=(batch, nb),
        in_specs=[
            pl.BlockSpec(memory_space=pltpu.SMEM),
            pl.BlockSpec((BLK, q_cols), cur),
            kv_spec(prev), kv_spec(cur), kv_spec(nxt),
            kv_spec(prev), kv_spec(cur), kv_spec(nxt),
        ],
        out_specs=pl.BlockSpec((BLK, q_cols), cur),
        out_shape=jax.ShapeDtypeStruct((t, q_cols), BF16),
        compiler_params=_compiler_params(("parallel", "parallel")),
        name="win_attn",
    )(sink, q, k, k, k, v, v, v)


def _proj_res_ln_kernel(a_ref, w_ref, x_ref, g_ref, b_ref, o_ref):
    h = jnp.dot(a_ref[...], w_ref[...], preferred_element_type=F32)
    o_ref[...] = _layer_norm_rows(ALPHA * x_ref[...] + h, g_ref[...], b_ref[...])


def _proj_res_ln(a, w, x, g, b, tm):
    t, kdim = a.shape
    row = lambda i: (i, 0)
    const = lambda i: (0, 0)
    return pl.pallas_call(
        _proj_res_ln_kernel,
        grid=(t // tm,),
        in_specs=[
            pl.BlockSpec((tm, kdim), row),
            pl.BlockSpec((kdim, D_MODEL), const),
            pl.BlockSpec((tm, D_MODEL), row),
            pl.BlockSpec((1, D_MODEL), const),
            pl.BlockSpec((1, D_MODEL), const),
        ],
        out_specs=pl.BlockSpec((tm, D_MODEL), row),
        out_shape=jax.ShapeDtypeStruct((t, D_MODEL), F32),
        compiler_params=_compiler_params(("parallel",)),
        name="proj_res_ln",
    )(a, w, x, g, b)


def _mlp_kernel(x_ref, wu_ref, wd_ref, g_ref, b_ref, o_ref, xb_ref):
    j = pl.program_id(1)

    @pl.when(j == 0)
    def _():
        xb_ref[...] = x_ref[...].astype(BF16)
        o_ref[...] = jnp.zeros_like(o_ref)

    h = jnp.dot(xb_ref[...], wu_ref[...], preferred_element_type=F32)
    h = jnp.square(jnp.maximum(h, 0.0)).astype(BF16)
    o_ref[...] += jnp.dot(h, wd_ref[...], preferred_element_type=F32)

    @pl.when(j == pl.num_programs(1) - 1)
    def _():
        o_ref[...] = _layer_norm_rows(ALPHA * x_ref[...] + o_ref[...], g_ref[...], b_ref[...])


def _mlp(x, wu, wd, g, b, tm, tf):
    t = x.shape[0]
    row = lambda i, j: (i, 0)
    const = lambda i, j: (0, 0)
    return pl.pallas_call(
        _mlp_kernel,
        grid=(t // tm, D_FF // tf),
        in_specs=[
            pl.BlockSpec((tm, D_MODEL), row),
            pl.BlockSpec((D_MODEL, tf), lambda i, j: (0, j)),
            pl.BlockSpec((tf, D_MODEL), lambda i, j: (j, 0)),
            pl.BlockSpec((1, D_MODEL), const),
            pl.BlockSpec((1, D_MODEL), const),
        ],
        out_specs=pl.BlockSpec((tm, D_MODEL), row),
        out_shape=jax.ShapeDtypeStruct((t, D_MODEL), F32),
        scratch_shapes=[pltpu.VMEM((tm, D_MODEL), BF16)],
        compiler_params=_compiler_params(("parallel", "arbitrary")),
        name="mlp",
    )(x, wu, wd, g, b)


def _mla_proj_kernel(x_ref, wd_ref, gq_ref, gkv_ref, wuq_ref, wuk_ref, wuvt_ref, tab_ref,
                     q_ref, k_ref, vt_ref):
    xb = x_ref[...].astype(BF16)
    tab = tab_ref[...]
    lane = lax.broadcasted_iota(jnp.int32, tab.shape, 1)

    def rope_pair(y):
        t = y * tab
        return jnp.where(lane < ROPE_DIM, t + pltpu.roll(t, ROPE_DIM, 1), 0.0).astype(BF16)

    lat = jnp.dot(xb, wd_ref[...], preferred_element_type=F32)
    cq = lat[:, :Q_LORA_PAD]
    cq = cq * lax.rsqrt(jnp.sum(cq * cq, axis=-1, keepdims=True) * (1.0 / Q_LORA) + RMS_EPS)
    cq = (cq * gq_ref[...]).astype(BF16)
    ckv = lat[:, Q_LORA_PAD:Q_LORA_PAD + KV_LORA]
    ckv = ckv * lax.rsqrt(jnp.mean(ckv * ckv, axis=-1, keepdims=True) + RMS_EPS)
    ckv = (ckv * gkv_ref[...]).astype(BF16)
    kr = rope_pair(lat[:, Q_LORA_PAD + KV_LORA:])

    heads_per_chunk = 2
    qchunk = heads_per_chunk * QK_SLAB
    for c in range(0, B_HEADS * QK_SLAB, qchunk):
        qq = jnp.dot(cq, wuq_ref[:, c:c + qchunk], preferred_element_type=F32)
        for hh in range(heads_per_chunk):
            o = hh * QK_SLAB
            q_ref[:, c + o:c + o + NOPE_DIM] = qq[:, o:o + NOPE_DIM].astype(BF16)
            q_ref[:, c + o + NOPE_DIM:c + o + QK_SLAB] = rope_pair(qq[:, o + NOPE_DIM:o + QK_SLAB])

    kchunk = 4 * NOPE_DIM
    for c in range(0, B_HEADS * NOPE_DIM, kchunk):
        kn = jnp.dot(ckv, wuk_ref[:, c:c + kchunk], preferred_element_type=F32)
        for hh in range(kchunk // NOPE_DIM):
            h = c // NOPE_DIM + hh
            k_ref[:, h * QK_SLAB:h * QK_SLAB + NOPE_DIM] = (
                kn[:, hh * NOPE_DIM:(hh + 1) * NOPE_DIM].astype(BF16))
            k_ref[:, h * QK_SLAB + NOPE_DIM:(h + 1) * QK_SLAB] = kr
    for c in range(0, B_HEADS * V_DIM, kchunk):
        vt_ref[c:c + kchunk, :] = _dot_nt(wuvt_ref[c:c + kchunk, :], ckv).astype(BF16)


def _mla_proj(x, wd, gq, gkv, wuq, wuk, wuvt, tab, seq, tm):
    t = x.shape[0]
    tiles_per_seq = seq // tm
    row = lambda i: (i, 0)
    const = lambda i: (0, 0)
    full = lambda a: pl.BlockSpec(a.shape, const)
    return pl.pallas_call(
        _mla_proj_kernel,
        grid=(t // tm,),
        in_specs=[
            pl.BlockSpec((tm, D_MODEL), row),
            full(wd), full(gq), full(gkv), full(wuq), full(wuk), full(wuvt),
            pl.BlockSpec((tm, LANES), lambda i: (i % tiles_per_seq, 0)),
        ],
        out_specs=[
            pl.BlockSpec((tm, B_HEADS * QK_SLAB), row),
            pl.BlockSpec((tm, B_HEADS * QK_SLAB), row),
            pl.BlockSpec((B_HEADS * V_DIM, tm), lambda i: (0, i)),
        ],
        out_shape=[
            jax.ShapeDtypeStruct((t, B_HEADS * QK_SLAB), BF16),
            jax.ShapeDtypeStruct((t, B_HEADS * QK_SLAB), BF16),
            jax.ShapeDtypeStruct((B_HEADS * V_DIM, t), BF16),
        ],
        compiler_params=_compiler_params(("parallel",)),
        name="mla_proj",
    )(x, wd, gq, gkv, wuq, wuk, wuvt, tab)


def _mla_attn_kernel(q_ref, k_ref, vt_ref, o_ref, s_ref, *, tk1, tk2):
    seq = k_ref.shape[0]
    tq = q_ref.shape[0]
    sub = 8
    q = q_ref[...]
    c_exp2 = math.log2(math.e) / math.sqrt(NOPE_DIM + ROPE_DIM)

    m8 = jnp.full((sub, tq), -jnp.inf, F32)
    for c0 in range(0, seq, tk1):
        s = _dot_nt(k_ref[c0:c0 + tk1, :], q) * c_exp2
        s_ref[c0:c0 + tk1, :] = s
        m8 = jnp.maximum(m8, jnp.max(s.reshape(tk1 // sub, sub, tq), axis=0))
    m = jnp.max(m8, axis=0, keepdims=True)

    l8 = jnp.zeros((sub, tq), F32)
    acc = jnp.zeros((V_DIM, tq), F32)
    for c0 in range(0, seq, tk2):
        p = jnp.exp2(s_ref[c0:c0 + tk2, :] - m)
        acc = acc + jnp.dot(vt_ref[:, c0:c0 + tk2], p.astype(BF16),
                            preferred_element_type=F32)
        l8 = l8 + jnp.sum(p.reshape(tk2 // sub, sub, tq), axis=0)
    inv_l = 1.0 / jnp.sum(l8, axis=0, keepdims=True)
    o_ref[...] = (acc * inv_l).T.astype(BF16)


def _mla_attn(q, k, vt, batch, seq, tq, tk1, tk2):
    t = q.shape[0]
    nq = seq // tq
    return pl.pallas_call(
        functools.partial(_mla_attn_kernel, tk1=tk1, tk2=tk2),
        grid=(batch, B_HEADS, nq),
        in_specs=[
            pl.BlockSpec((tq, QK_SLAB), lambda b, h, i: (b * nq + i, h)),
            pl.BlockSpec((seq, QK_SLAB), lambda b, h, i: (b, h)),
            pl.BlockSpec((V_DIM, seq), lambda b, h, i: (h, b)),
        ],
        out_specs=pl.BlockSpec((tq, V_DIM), lambda b, h, i: (b * nq + i, h)),
        out_shape=jax.ShapeDtypeStruct((t, B_HEADS * V_DIM), BF16),
        scratch_shapes=[pltpu.VMEM((seq, tq), F32)],
        compiler_params=_compiler_params(("parallel", "parallel", "arbitrary")),
        name="mla_attn",
    )(q, k, vt)


def _rope_angles(seq, dim):
    inv = 1.0 / (ROPE_THETA ** (jnp.arange(0, dim, 2, dtype=F32) / dim))
    ang = jnp.arange(seq, dtype=F32)[:, None] * inv[None, :]
    return jnp.concatenate([ang, ang], axis=-1)


def _rot_half_cols(w):
    half = w.shape[-1] // 2
    return jnp.concatenate([-w[:, half:], w[:, :half]], axis=-1)


def _prep_mla_weights(w_dqkv, q_norm, kv_norm, w_uq, w_ukv):
    wq = w_dqkv[:, :Q_LORA]
    wkv = w_dqkv[:, Q_LORA:Q_LORA + KV_LORA]
    wr = w_dqkv[:, Q_LORA + KV_LORA:]
    zpad = jnp.zeros((D_MODEL, Q_LORA_PAD - Q_LORA), w_dqkv.dtype)
    wd = jnp.concatenate([wq, zpad, wkv, wr, _rot_half_cols(wr)], axis=1).astype(BF16)
    gq = jnp.pad(q_norm, (0, Q_LORA_PAD - Q_LORA)).reshape(1, Q_LORA_PAD)
    gkv = kv_norm.reshape(1, KV_LORA)
    uq = w_uq.reshape(Q_LORA, B_HEADS, NOPE_DIM + ROPE_DIM)
    uq_r = uq[:, :, NOPE_DIM:]
    uq_rot = jnp.concatenate([-uq_r[..., ROPE_DIM // 2:], uq_r[..., :ROPE_DIM // 2]], axis=-1)
    wuq = jnp.concatenate([uq, uq_rot], axis=-1).reshape(Q_LORA, B_HEADS * QK_SLAB)
    wuq = jnp.pad(wuq, ((0, Q_LORA_PAD - Q_LORA), (0, 0))).astype(BF16)
    ukv = w_ukv.reshape(KV_LORA, B_HEADS, NOPE_DIM + V_DIM)
    wuk = ukv[:, :, :NOPE_DIM].reshape(KV_LORA, B_HEADS * NOPE_DIM).astype(BF16)
    wuvt = ukv[:, :, NOPE_DIM:].reshape(KV_LORA, B_HEADS * V_DIM).T.astype(BF16)
    return wd, gq, gkv, wuq, wuk, wuvt


def _tiles(seq):
    tm = 512
    assert seq % tm == 0 and seq % BLK == 0
    return dict(tm=tm, tf=1024, tq=512, tk1=min(seq, 2048), tk2=512)


def _trunk(x3, weights):
    batch, seq, _ = x3.shape
    cfg = _tiles(seq)
    tm = cfg["tm"]
    x = x3.reshape(batch * seq, D_MODEL)

    ang_a = _rope_angles(seq, A_HEAD_DIM)
    sign = jnp.where(jnp.arange(A_HEAD_DIM) < A_HEAD_DIM // 2, -1.0, 1.0).astype(F32)
    cos_a, sin_a = jnp.cos(ang_a), jnp.sin(ang_a) * sign
    ang_b = _rope_angles(seq, ROPE_DIM)
    tab_b = jnp.concatenate([jnp.cos(ang_b), jnp.sin(ang_b)], axis=-1)

    for i in range(DEPTH):
        j = i // N_MIXERS
        if i % N_MIXERS == 0:
            q, k, v = _qkv_rope(x, weights["w_qkv_a"][j], cos_a, sin_a, seq, tm)
            attn = _win_attn(q, k, v, weights["sink_a"][j], batch, seq)
            w_o = weights["w_o_a"][j]
        else:
            q, k, v = _mla_proj(x, *weights["mla"][j], tab_b, seq, tm)
            attn = _mla_attn(q, k, v, batch, seq, cfg["tq"], cfg["tk1"], cfg["tk2"])
            w_o = weights["w_o_b"][j]
        g, b = weights["ln_g"], weights["ln_b"]
        x = _proj_res_ln(attn, w_o, x, g[i, 0:1], b[i, 0:1], tm)
        x = _mlp(x, weights["w_up"][i], weights["w_down"][i], g[i, 1:2], b[i, 1:2],
                 tm, cfg["tf"])
    return x.reshape(batch, seq, D_MODEL)


def kernel(x_prompt, x_sample, w_qkv_a, sink_a, w_o_a, w_dqkv_b, q_norm_b, kv_norm_b,
           w_uq_b, w_ukv_b, w_o_b, w_up, w_down, ln_g, ln_b):
    weights = dict(
        w_qkv_a=w_qkv_a.astype(BF16),
        sink_a=sink_a.astype(F32),
        w_o_a=w_o_a.astype(BF16),
        mla=[_prep_mla_weights(w_dqkv_b[j], q_norm_b[j], kv_norm_b[j], w_uq_b[j], w_ukv_b[j])
             for j in range(w_dqkv_b.shape[0])],
        w_o_b=w_o_b.astype(BF16),
        w_up=w_up.astype(BF16),
        w_down=w_down.astype(BF16),
        ln_g=ln_g.astype(F32),
        ln_b=ln_b.astype(F32),
    )
    return (_trunk(x_prompt, weights), _trunk(x_sample, weights))
```

```python
import functools
import math

import jax
import jax.numpy as jnp
from jax import lax
from jax.experimental import pallas as pl
from jax.experimental.pallas import tpu as pltpu

D_MODEL = 2048
DEPTH = 4
N_MIXERS = 2
BLK = 128
A_HEADS = 16
A_KV_HEADS = 4
A_GROUP = A_HEADS // A_KV_HEADS
A_HEAD_DIM = D_MODEL // A_HEADS
WINDOW = 128
B_HEADS = 16
Q_LORA = 448
KV_LORA = 128
NOPE_DIM = 128
ROPE_DIM = 64
V_DIM = 128
D_FF = 4 * D_MODEL
ROPE_THETA = 10000.0
LN_EPS = 1e-5
RMS_EPS = 1e-6
ALPHA = (2 * DEPTH) ** 0.25

LANES = 128
MXU_DIM = 256
VMEM_LIMIT_BYTES = 56 * 1024 * 1024

Q_LORA_PAD = 512
LAT_COLS = Q_LORA_PAD + KV_LORA + 2 * ROPE_DIM
QK_SLAB = MXU_DIM

F32 = jnp.float32
BF16 = jnp.bfloat16


def _compiler_params(semantics):
    return pltpu.CompilerParams(dimension_semantics=semantics,
                                vmem_limit_bytes=VMEM_LIMIT_BYTES)


def _layer_norm_rows(y, g, b):
    mu = jnp.mean(y, axis=-1, keepdims=True)
    d = y - mu
    var = jnp.mean(d * d, axis=-1, keepdims=True)
    return d * lax.rsqrt(var + LN_EPS) * g + b


def _dot_nt(a, b):
    return lax.dot_general(a, b, (((1,), (1,)), ((), ())), preferred_element_type=F32)


def _qkv_rope_kernel(x_ref, w_ref, cos_ref, sin_ref, q_ref, k_ref, v_ref):
    xb = x_ref[...].astype(BF16)
    cos = cos_ref[...]
    sin = sin_ref[...]
    chunk = A_GROUP * A_HEAD_DIM
    q_cols = A_HEADS * A_HEAD_DIM
    k_cols = A_KV_HEADS * A_HEAD_DIM

    def rope(t):
        return t * cos + pltpu.roll(t, A_HEAD_DIM // 2, 1) * sin

    for c in range(0, q_cols + k_cols, chunk):
        acc = jnp.dot(xb, w_ref[:, c:c + chunk], preferred_element_type=F32)
        for hh in range(chunk // A_HEAD_DIM):
            y = rope(acc[:, hh * A_HEAD_DIM:(hh + 1) * A_HEAD_DIM]).astype(BF16)
            col = c + hh * A_HEAD_DIM
            if col < q_cols:
                q_ref[:, col:col + A_HEAD_DIM] = y
            else:
                k_ref[:, col - q_cols:col - q_cols + A_HEAD_DIM] = y
    v0 = q_cols + k_cols
    v_ref[...] = jnp.dot(xb, w_ref[:, v0:v0 + k_cols],
                         preferred_element_type=F32).astype(BF16)


def _qkv_rope(x, w, cos, sin, seq, tm):
    t = x.shape[0]
    tiles_per_seq = seq // tm
    q_cols = A_HEADS * A_HEAD_DIM
    k_cols = A_KV_HEADS * A_HEAD_DIM
    row = lambda i: (i, 0)
    pos = lambda i: (i % tiles_per_seq, 0)
    return pl.pallas_call(
        _qkv_rope_kernel,
        grid=(t // tm,),
        in_specs=[
            pl.BlockSpec((tm, D_MODEL), row),
            pl.BlockSpec((D_MODEL, q_cols + 2 * k_cols), lambda i: (0, 0)),
            pl.BlockSpec((tm, A_HEAD_DIM), pos),
            pl.BlockSpec((tm, A_HEAD_DIM), pos),
        ],
        out_specs=[
            pl.BlockSpec((tm, q_cols), row),
            pl.BlockSpec((tm, k_cols), row),
            pl.BlockSpec((tm, k_cols), row),
        ],
        out_shape=[
            jax.ShapeDtypeStruct((t, q_cols), BF16),
            jax.ShapeDtypeStruct((t, k_cols), BF16),
            jax.ShapeDtypeStruct((t, k_cols), BF16),
        ],
        compiler_params=_compiler_params(("parallel",)),
        name="qkv_rope",
    )(x, w, cos, sin)


def _win_attn_kernel(sink_ref, q_ref, kp_ref, kc_ref, kn_ref, vp_ref, vc_ref, vn_ref,
                     o_ref, *, nb):
    n = pl.program_id(1)
    rows = A_GROUP * BLK
    span = BLK + 2 * WINDOW
    qi = lax.broadcasted_iota(jnp.int32, (rows, span), 0) & (BLK - 1)
    col = lax.broadcasted_iota(jnp.int32, (rows, span), 1)
    lo = jnp.where(n > 0, 0, WINDOW)
    hi = jnp.where(n < nb - 1, span, BLK + WINDOW)
    valid = (col >= jnp.maximum(qi, lo)) & (col <= qi + 2 * WINDOW) & (col < hi)
    bias = jnp.where(valid, 0.0, -jnp.inf).astype(F32)
    grp = lax.broadcasted_iota(jnp.int32, (rows, 1), 0) // BLK
    scale = 1.0 / math.sqrt(A_HEAD_DIM)

    for h in range(A_KV_HEADS):
        ks = slice(h * A_HEAD_DIM, (h + 1) * A_HEAD_DIM)
        qh = jnp.concatenate(
            [q_ref[:, (h * A_GROUP + g) * A_HEAD_DIM:(h * A_GROUP + g + 1) * A_HEAD_DIM]
             for g in range(A_GROUP)], axis=0)
        kh = jnp.concatenate([kp_ref[:, ks], kc_ref[:, ks], kn_ref[:, ks]], axis=0)
        vh = jnp.concatenate([vp_ref[:, ks], vc_ref[:, ks], vn_ref[:, ks]], axis=0)
        s = _dot_nt(qh, kh) * scale + bias
        sink = jnp.full((rows, 1), sink_ref[h * A_GROUP], F32)
        for g in range(1, A_GROUP):
            sink = jnp.where(grp == g, sink_ref[h * A_GROUP + g], sink)
        m = jnp.maximum(jnp.max(s, axis=-1, keepdims=True), sink)
        p = jnp.exp(s - m)
        denom = jnp.sum(p, axis=-1, keepdims=True) + jnp.exp(sink - m)
        o = jnp.dot((p / denom).astype(BF16), vh, preferred_element_type=F32)
        for g in range(A_GROUP):
            c0 = (h * A_GROUP + g) * A_HEAD_DIM
            o_ref[:, c0:c0 + A_HEAD_DIM] = o[g * BLK:(g + 1) * BLK].astype(BF16)


def _win_attn(q, k, v, sink, batch, seq):
    t = q.shape[0]
    nb = seq // BLK
    q_cols = A_HEADS * A_HEAD_DIM
    k_cols = A_KV_HEADS * A_HEAD_DIM
    cur = lambda b, n: (b * nb + n, 0)
    prev = lambda b, n: (b * nb + jnp.maximum(n - 1, 0), 0)
    nxt = lambda b, n: (b * nb + jnp.minimum(n + 1, nb - 1), 0)
    kv_spec = lambda im: pl.BlockSpec((BLK, k_cols), im)
    return pl.pallas_call(
        functools.partial(_win_attn_kernel, nb=nb),
        grid=(batch, nb),
        in_specs=[
            pl.BlockSpec(memory_space=pltpu.SMEM),
            pl.BlockSpec((BLK, q_cols), cur),
            kv_spec(prev), kv_spec(cur), kv_spec(nxt),
            kv_spec(prev), kv_spec(cur), kv_spec(nxt),
        ],
        out_specs=pl.BlockSpec((BLK, q_cols), cur),
        out_shape=jax.ShapeDtypeStruct((t, q_cols), BF16),
        compiler_params=_compiler_params(("parallel", "parallel")),
        name="win_attn",
    )(sink, q, k, k, k, v, v, v)


def _proj_res_ln_kernel(a_ref, w_ref, x_ref, g_ref, b_ref, o_ref):
    h = jnp.dot(a_ref[...], w_ref[...], preferred_element_type=F32)
    o_ref[...] = _layer_norm_rows(ALPHA * x_ref[...] + h, g_ref[...], b_ref[...])


def _proj_res_ln(a, w, x, g, b, tm):
    t, kdim = a.shape
    row = lambda i: (i, 0)
    const = lambda i: (0, 0)
    return pl.pallas_call(
        _proj_res_ln_kernel,
        grid=(t // tm,),
        in_specs=[
            pl.BlockSpec((tm, kdim), row),
            pl.BlockSpec((kdim, D_MODEL), const),
            pl.BlockSpec((tm, D_MODEL), row),
            pl.BlockSpec((1, D_MODEL), const),
            pl.BlockSpec((1, D_MODEL), const),
        ],
        out_specs=pl.BlockSpec((tm, D_MODEL), row),
        out_shape=jax.ShapeDtypeStruct((t, D_MODEL), F32),
        compiler_params=_compiler_params(("parallel",)),
        name="proj_res_ln",
    )(a, w, x, g, b)


def _mlp_kernel(x_ref, wu_ref, wd_ref, g_ref, b_ref, o_ref, xb_ref):
    j = pl.program_id(1)

    @pl.when(j == 0)
    def _():
        xb_ref[...] = x_ref[...].astype(BF16)
        o_ref[...] = jnp.zeros_like(o_ref)

    h = jnp.dot(xb_ref[...], wu_ref[...], preferred_element_type=F32)
    h = jnp.square(jnp.maximum(h, 0.0)).astype(BF16)
    o_ref[...] += jnp.dot(h, wd_ref[...], preferred_element_type=F32)

    @pl.when(j == pl.num_programs(1) - 1)
    def _():
        o_ref[...] = _layer_norm_rows(ALPHA * x_ref[...] + o_ref[...], g_ref[...], b_ref[...])


def _mlp(x, wu, wd, g, b, tm, tf):
    t = x.shape[0]
    row = lambda i, j: (i, 0)
    const = lambda i, j: (0, 0)
    return pl.pallas_call(
        _mlp_kernel,
        grid=(t // tm, D_FF // tf),
        in_specs=[
            pl.BlockSpec((tm, D_MODEL), row),
            pl.BlockSpec((D_MODEL, tf), lambda i, j: (0, j)),
            pl.BlockSpec((tf, D_MODEL), lambda i, j: (j, 0)),
            pl.BlockSpec((1, D_MODEL), const),
            pl.BlockSpec((1, D_MODEL), const),
        ],
        out_specs=pl.BlockSpec((tm, D_MODEL), row),
        out_shape=jax.ShapeDtypeStruct((t, D_MODEL), F32),
        scratch_shapes=[pltpu.VMEM((tm, D_MODEL), BF16)],
        compiler_params=_compiler_params(("parallel", "arbitrary")),
        name="mlp",
    )(x, wu, wd, g, b)


def _mla_proj_kernel(x_ref, wd_ref, gq_ref, gkv_ref, wuq_ref, wuk_ref, wuvt_ref, tab_ref,
                     q_ref, k_ref, vt_ref):
    xb = x_ref[...].astype(BF16)
    tab = tab_ref[...]
    lane = lax.broadcasted_iota(jnp.int32, tab.shape, 1)

    def rope_pair(y):
        t = y * tab
        return jnp.where(lane < ROPE_DIM, t + pltpu.roll(t, ROPE_DIM, 1), 0.0).astype(BF16)

    lat = jnp.dot(xb, wd_ref[...], preferred_element_type=F32)
    cq = lat[:, :Q_LORA_PAD]
    cq = cq * lax.rsqrt(jnp.sum(cq * cq, axis=-1, keepdims=True) * (1.0 / Q_LORA) + RMS_EPS)
    cq = (cq * gq_ref[...]).astype(BF16)
    ckv = lat[:, Q_LORA_PAD:Q_LORA_PAD + KV_LORA]
    ckv = ckv * lax.rsqrt(jnp.mean(ckv * ckv, axis=-1, keepdims=True) + RMS_EPS)
    ckv = (ckv * gkv_ref[...]).astype(BF16)
    kr = rope_pair(lat[:, Q_LORA_PAD + KV_LORA:])

    heads_per_chunk = 2
    qchunk = heads_per_chunk * QK_SLAB
    for c in range(0, B_HEADS * QK_SLAB, qchunk):
        qq = jnp.dot(cq, wuq_ref[:, c:c + qchunk], preferred_element_type=F32)
        for hh in range(heads_per_chunk):
            o = hh * QK_SLAB
            q_ref[:, c + o:c + o + NOPE_DIM] = qq[:, o:o + NOPE_DIM].astype(BF16)
            q_ref[:, c + o + NOPE_DIM:c + o + QK_SLAB] = rope_pair(qq[:, o + NOPE_DIM:o + QK_SLAB])

    kchunk = 4 * NOPE_DIM
    for c in range(0, B_HEADS * NOPE_DIM, kchunk):
        kn = jnp.dot(ckv, wuk_ref[:, c:c + kchunk], preferred_element_type=F32)
        for hh in range(kchunk // NOPE_DIM):
            h = c // NOPE_DIM + hh
            k_ref[:, h * QK_SLAB:h * QK_SLAB + NOPE_DIM] = (
                kn[:, hh * NOPE_DIM:(hh + 1) * NOPE_DIM].astype(BF16))
            k_ref[:, h * QK_SLAB + NOPE_DIM:(h + 1) * QK_SLAB] = kr
    for c in range(0, B_HEADS * V_DIM, kchunk):
        vt_ref[c:c + kchunk, :] = _dot_nt(wuvt_ref[c:c + kchunk, :], ckv).astype(BF16)


def _mla_proj(x, wd, gq, gkv, wuq, wuk, wuvt, tab, seq, tm):
    t = x.shape[0]
    tiles_per_seq = seq // tm
    row = lambda i: (i, 0)
    const = lambda i: (0, 0)
    full = lambda a: pl.BlockSpec(a.shape, const)
    return pl.pallas_call(
        _mla_proj_kernel,
        grid=(t // tm,),
        in_specs=[
            pl.BlockSpec((tm, D_MODEL), row),
            full(wd), full(gq), full(gkv), full(wuq), full(wuk), full(wuvt),
            pl.BlockSpec((tm, LANES), lambda i: (i % tiles_per_seq, 0)),
        ],
        out_specs=[
            pl.BlockSpec((tm, B_HEADS * QK_SLAB), row),
            pl.BlockSpec((tm, B_HEADS * QK_SLAB), row),
            pl.BlockSpec((B_HEADS * V_DIM, tm), lambda i: (0, i)),
        ],
        out_shape=[
            jax.ShapeDtypeStruct((t, B_HEADS * QK_SLAB), BF16),
            jax.ShapeDtypeStruct((t, B_HEADS * QK_SLAB), BF16),
            jax.ShapeDtypeStruct((B_HEADS * V_DIM, t), BF16),
        ],
        compiler_params=_compiler_params(("parallel",)),
        name="mla_proj",
    )(x, wd, gq, gkv, wuq, wuk, wuvt, tab)


F32_SUBLANES = 8
MLA_EXP2_SCALE = math.log2(math.e) / math.sqrt(NOPE_DIM + ROPE_DIM)


def _mla_attn_kernel(q_ref, k_ref, vt_ref, o_ref, s_ref, acc_ref, *, tk1, tk2, unroll):
    seq = k_ref.shape[0]
    tq = q_ref.shape[0]
    sub = F32_SUBLANES
    q = q_ref[...]

    m8 = jnp.full((sub, tq), -jnp.inf, F32)
    for c0 in range(0, seq, tk1):
        s = _dot_nt(k_ref[c0:c0 + tk1, :], q) * MLA_EXP2_SCALE
        s_ref[c0:c0 + tk1, :] = s
        m8 = jnp.maximum(m8, jnp.max(s.reshape(tk1 // sub, sub, tq), axis=0))
    m = jnp.max(m8, axis=0, keepdims=True)

    acc_ref[...] = jnp.zeros_like(acc_ref)

    def body(c, l8):
        start = pl.multiple_of(c * tk2, tk2)
        p = jnp.exp2(s_ref[pl.ds(start, tk2), :] - m)
        acc_ref[...] += jnp.dot(vt_ref[:, pl.ds(start, tk2)], p.astype(BF16),
                                preferred_element_type=F32)
        return l8 + jnp.sum(p.reshape(tk2 // sub, sub, tq), axis=0)

    l8 = lax.fori_loop(0, seq // tk2, body, jnp.zeros((sub, tq), F32), unroll=unroll)
    inv_l = 1.0 / jnp.sum(l8, axis=0, keepdims=True)
    o_ref[...] = (acc_ref[...] * inv_l).T.astype(BF16)


def _mla_attn(q, k, vt, batch, seq, tq, tk1, tk2, unroll):
    t = q.shape[0]
    nq = seq // tq
    return pl.pallas_call(
        functools.partial(_mla_attn_kernel, tk1=tk1, tk2=tk2, unroll=unroll),
        grid=(batch, B_HEADS, nq),
        in_specs=[
            pl.BlockSpec((tq, QK_SLAB), lambda b, h, i: (b * nq + i, h)),
            pl.BlockSpec((seq, QK_SLAB), lambda b, h, i: (b, h)),
            pl.BlockSpec((V_DIM, seq), lambda b, h, i: (h, b)),
        ],
        out_specs=pl.BlockSpec((tq, V_DIM), lambda b, h, i: (b * nq + i, h)),
        out_shape=jax.ShapeDtypeStruct((t, B_HEADS * V_DIM), BF16),
        scratch_shapes=[
            pltpu.VMEM((seq, tq), F32),
            pltpu.VMEM((V_DIM, tq), F32),
        ],
        compiler_params=_compiler_params(("parallel", "parallel", "arbitrary")),
        name="mla_attn",
    )(q, k, vt)


def _rope_angles(seq, dim):
    inv = 1.0 / (ROPE_THETA ** (jnp.arange(0, dim, 2, dtype=F32) / dim))
    ang = jnp.arange(seq, dtype=F32)[:, None] * inv[None, :]
    return jnp.concatenate([ang, ang], axis=-1)


def _rot_half_cols(w):
    half = w.shape[-1] // 2
    return jnp.concatenate([-w[:, half:], w[:, :half]], axis=-1)


def _prep_mla_weights(w_dqkv, q_norm, kv_norm, w_uq, w_ukv):
    wq = w_dqkv[:, :Q_LORA]
    wkv = w_dqkv[:, Q_LORA:Q_LORA + KV_LORA]
    wr = w_dqkv[:, Q_LORA + KV_LORA:]
    zpad = jnp.zeros((D_MODEL, Q_LORA_PAD - Q_LORA), w_dqkv.dtype)
    wd = jnp.concatenate([wq, zpad, wkv, wr, _rot_half_cols(wr)], axis=1).astype(BF16)
    gq = jnp.pad(q_norm, (0, Q_LORA_PAD - Q_LORA)).reshape(1, Q_LORA_PAD)
    gkv = kv_norm.reshape(1, KV_LORA)
    uq = w_uq.reshape(Q_LORA, B_HEADS, NOPE_DIM + ROPE_DIM)
    uq_r = uq[:, :, NOPE_DIM:]
    uq_rot = jnp.concatenate([-uq_r[..., ROPE_DIM // 2:], uq_r[..., :ROPE_DIM // 2]], axis=-1)
    wuq = jnp.concatenate([uq, uq_rot], axis=-1).reshape(Q_LORA, B_HEADS * QK_SLAB)
    wuq = jnp.pad(wuq, ((0, Q_LORA_PAD - Q_LORA), (0, 0))).astype(BF16)
    ukv = w_ukv.reshape(KV_LORA, B_HEADS, NOPE_DIM + V_DIM)
    wuk = ukv[:, :, :NOPE_DIM].reshape(KV_LORA, B_HEADS * NOPE_DIM).astype(BF16)
    wuvt = ukv[:, :, NOPE_DIM:].reshape(KV_LORA, B_HEADS * V_DIM).T.astype(BF16)
    return wd, gq, gkv, wuq, wuk, wuvt


def _tiles(seq):
    tm = 512
    assert seq % tm == 0 and seq % BLK == 0
    tk2 = 512
    unroll = max(1, min(8, seq // tk2 // 2))
    return dict(tm=tm, tf=1024, tq=512, tk1=min(seq, 2048), tk2=tk2, unroll=unroll)


def _trunk(x3, weights):
    batch, seq, _ = x3.shape
    cfg = _tiles(seq)
    tm = cfg["tm"]
    x = x3.reshape(batch * seq, D_MODEL)

    ang_a = _rope_angles(seq, A_HEAD_DIM)
    sign = jnp.where(jnp.arange(A_HEAD_DIM) < A_HEAD_DIM // 2, -1.0, 1.0).astype(F32)
    cos_a, sin_a = jnp.cos(ang_a), jnp.sin(ang_a) * sign
    ang_b = _rope_angles(seq, ROPE_DIM)
    tab_b = jnp.concatenate([jnp.cos(ang_b), jnp.sin(ang_b)], axis=-1)

    for i in range(DEPTH):
        j = i // N_MIXERS
        if i % N_MIXERS == 0:
            q, k, v = _qkv_rope(x, weights["w_qkv_a"][j], cos_a, sin_a, seq, tm)
            attn = _win_attn(q, k, v, weights["sink_a"][j], batch, seq)
            w_o = weights["w_o_a"][j]
        else:
            q, k, v = _mla_proj(x, *weights["mla"][j], tab_b, seq, tm)
            attn = _mla_attn(q, k, v, batch, seq, cfg["tq"], cfg["tk1"], cfg["tk2"],
                             cfg["unroll"])
            w_o = weights["w_o_b"][j]
        g, b = weights["ln_g"], weights["ln_b"]
        x = _proj_res_ln(attn, w_o, x, g[i, 0:1], b[i, 0:1], tm)
        x = _mlp(x, weights["w_up"][i], weights["w_down"][i], g[i, 1:2], b[i, 1:2],
                 tm, cfg["tf"])
    return x.reshape(batch, seq, D_MODEL)


def kernel(x_prompt, x_sample, w_qkv_a, sink_a, w_o_a, w_dqkv_b, q_norm_b, kv_norm_b,
           w_uq_b, w_ukv_b, w_o_b, w_up, w_down, ln_g, ln_b):
    weights = dict(
        w_qkv_a=w_qkv_a.astype(BF16),
        sink_a=sink_a.astype(F32),
        w_o_a=w_o_a.astype(BF16),
        mla=[_prep_mla_weights(w_dqkv_b[j], q_norm_b[j], kv_norm_b[j], w_uq_b[j], w_ukv_b[j])
             for j in range(w_dqkv_b.shape[0])],
        w_o_b=w_o_b.astype(BF16),
        w_up=w_up.astype(BF16),
        w_down=w_down.astype(BF16),
        ln_g=ln_g.astype(F32),
        ln_b=ln_b.astype(F32),
    )
    return (_trunk(x_prompt, weights), _trunk(x_sample, weights))
```

```python
import functools
import math

import jax
import jax.numpy as jnp
from jax import lax
from jax.experimental import pallas as pl
from jax.experimental.pallas import tpu as pltpu

D_MODEL = 2048
DEPTH = 4
N_MIXERS = 2
BLK = 128
A_HEADS = 16
A_KV_HEADS = 4
A_GROUP = A_HEADS // A_KV_HEADS
A_HEAD_DIM = D_MODEL // A_HEADS
WINDOW = 128
B_HEADS = 16
Q_LORA = 448
KV_LORA = 128
NOPE_DIM = 128
ROPE_DIM = 64
V_DIM = 128
D_FF = 4 * D_MODEL
ROPE_THETA = 10000.0
LN_EPS = 1e-5
RMS_EPS = 1e-6
ALPHA = (2 * DEPTH) ** 0.25

LANES = 128
MXU_DIM = 256
VMEM_LIMIT_BYTES = 56 * 1024 * 1024

Q_LORA_PAD = 512
LAT_COLS = Q_LORA_PAD + KV_LORA + 2 * ROPE_DIM
QK_SLAB = MXU_DIM

F32 = jnp.float32
BF16 = jnp.bfloat16


def _compiler_params(semantics):
    return pltpu.CompilerParams(dimension_semantics=semantics,
                                vmem_limit_bytes=VMEM_LIMIT_BYTES)


def _layer_norm_rows(y, g, b):
    mu = jnp.mean(y, axis=-1, keepdims=True)
    d = y - mu
    var = jnp.mean(d * d, axis=-1, keepdims=True)
    return d * lax.rsqrt(var + LN_EPS) * g + b


def _dot_nt(a, b):
    return lax.dot_general(a, b, (((1,), (1,)), ((), ())), preferred_element_type=F32)


def _qkv_rope_kernel(x_ref, w_ref, cos_ref, sin_ref, q_ref, k_ref, v_ref):
    xb = x_ref[...].astype(BF16)
    cos = cos_ref[...]
    sin = sin_ref[...]
    chunk = A_GROUP * A_HEAD_DIM
    q_cols = A_HEADS * A_HEAD_DIM
    k_cols = A_KV_HEADS * A_HEAD_DIM

    def rope(t):
        return t * cos + pltpu.roll(t, A_HEAD_DIM // 2, 1) * sin

    for c in range(0, q_cols + k_cols, chunk):
        acc = jnp.dot(xb, w_ref[:, c:c + chunk], preferred_element_type=F32)
        for hh in range(chunk // A_HEAD_DIM):
            y = rope(acc[:, hh * A_HEAD_DIM:(hh + 1) * A_HEAD_DIM]).astype(BF16)
            col = c + hh * A_HEAD_DIM
            if col < q_cols:
                q_ref[:, col:col + A_HEAD_DIM] = y
            else:
                k_ref[:, col - q_cols:col - q_cols + A_HEAD_DIM] = y
    v0 = q_cols + k_cols
    v_ref[...] = jnp.dot(xb, w_ref[:, v0:v0 + k_cols],
                         preferred_element_type=F32).astype(BF16)


def _qkv_rope(x, w_stack, layer, cos, sin, seq, tm):
    t = x.shape[0]
    tiles_per_seq = seq // tm
    q_cols = A_HEADS * A_HEAD_DIM
    k_cols = A_KV_HEADS * A_HEAD_DIM
    row = lambda i: (i, 0)
    pos = lambda i: (i % tiles_per_seq, 0)
    return pl.pallas_call(
        _qkv_rope_kernel,
        grid=(t // tm,),
        in_specs=[
            pl.BlockSpec((tm, D_MODEL), row),
            pl.BlockSpec((None, D_MODEL, q_cols + 2 * k_cols), lambda i: (layer, 0, 0)),
            pl.BlockSpec((tm, A_HEAD_DIM), pos),
            pl.BlockSpec((tm, A_HEAD_DIM), pos),
        ],
        out_specs=[
            pl.BlockSpec((tm, q_cols), row),
            pl.BlockSpec((tm, k_cols), row),
            pl.BlockSpec((tm, k_cols), row),
        ],
        out_shape=[
            jax.ShapeDtypeStruct((t, q_cols), BF16),
            jax.ShapeDtypeStruct((t, k_cols), BF16),
            jax.ShapeDtypeStruct((t, k_cols), BF16),
        ],
        compiler_params=_compiler_params(("parallel",)),
        name="qkv_rope",
    )(x, w_stack, cos, sin)


def _win_attn_kernel(sink_ref, q_ref, kp_ref, kc_ref, kn_ref, vp_ref, vc_ref, vn_ref,
                     o_ref, *, nb):
    n = pl.program_id(1)
    rows = A_GROUP * BLK
    span = BLK + 2 * WINDOW
    qi = lax.broadcasted_iota(jnp.int32, (rows, span), 0) & (BLK - 1)
    col = lax.broadcasted_iota(jnp.int32, (rows, span), 1)
    lo = jnp.where(n > 0, 0, WINDOW)
    hi = jnp.where(n < nb - 1, span, BLK + WINDOW)
    valid = (col >= jnp.maximum(qi, lo)) & (col <= qi + 2 * WINDOW) & (col < hi)
    bias = jnp.where(valid, 0.0, -jnp.inf).astype(F32)
    grp = lax.broadcasted_iota(jnp.int32, (rows, 1), 0) // BLK
    log2e = math.log2(math.e)
    scale2 = log2e / math.sqrt(A_HEAD_DIM)

    for h in range(A_KV_HEADS):
        ks = slice(h * A_HEAD_DIM, (h + 1) * A_HEAD_DIM)
        qh = jnp.concatenate(
            [q_ref[:, (h * A_GROUP + g) * A_HEAD_DIM:(h * A_GROUP + g + 1) * A_HEAD_DIM]
             for g in range(A_GROUP)], axis=0)
        kh = jnp.concatenate([kp_ref[:, ks], kc_ref[:, ks], kn_ref[:, ks]], axis=0)
        vh = jnp.concatenate([vp_ref[:, ks], vc_ref[:, ks], vn_ref[:, ks]], axis=0)
        s = _dot_nt(qh, kh) * scale2 + bias
        sink = jnp.full((rows, 1), sink_ref[h * A_GROUP] * log2e, F32)
        for g in range(1, A_GROUP):
            sink = jnp.where(grp == g, sink_ref[h * A_GROUP + g] * log2e, sink)
        m = jnp.maximum(jnp.max(s, axis=-1, keepdims=True), sink)
        p = jnp.exp2(s - m)
        denom = jnp.sum(p, axis=-1, keepdims=True) + jnp.exp2(sink - m)
        o = jnp.dot(p.astype(BF16), vh, preferred_element_type=F32) * (1.0 / denom)
        for g in range(A_GROUP):
            c0 = (h * A_GROUP + g) * A_HEAD_DIM
            o_ref[:, c0:c0 + A_HEAD_DIM] = o[g * BLK:(g + 1) * BLK].astype(BF16)


def _win_attn(q, k, v, sink, batch, seq):
    t = q.shape[0]
    nb = seq // BLK
    q_cols = A_HEADS * A_HEAD_DIM
    k_cols = A_KV_HEADS * A_HEAD_DIM
    cur = lambda b, n: (b * nb + n, 0)
    prev = lambda b, n: (b * nb + jnp.maximum(n - 1, 0), 0)
    nxt = lambda b, n: (b * nb + jnp.minimum(n + 1, nb - 1), 0)
    kv_spec = lambda im: pl.BlockSpec((BLK, k_cols), im)
    return pl.pallas_call(
        functools.partial(_win_attn_kernel, nb=nb),
        grid=(batch, nb),
        in_specs=[
            pl.BlockSpec(memory_space=pltpu.SMEM),
            pl.BlockSpec((BLK, q_cols), cur),
            kv_spec(prev), kv_spec(cur), kv_spec(nxt),
            kv_spec(prev), kv_spec(cur), kv_spec(nxt),
        ],
        out_specs=pl.BlockSpec((BLK, q_cols), cur),
        out_shape=jax.ShapeDtypeStruct((t, q_cols), BF16),
        compiler_params=_compiler_params(("parallel", "parallel")),
        name="win_attn",
    )(sink, q, k, k, k, v, v, v)


def _proj_res_ln_kernel(a_ref, w_ref, x_ref, g_ref, b_ref, o_ref):
    h = jnp.dot(a_ref[...], w_ref[...], preferred_element_type=F32)
    o_ref[...] = _layer_norm_rows(ALPHA * x_ref[...] + h, g_ref[...], b_ref[...])


def _proj_res_ln(a, w_stack, layer, x, g, b, tm):
    t, kdim = a.shape
    row = lambda i: (i, 0)
    const = lambda i: (0, 0)
    return pl.pallas_call(
        _proj_res_ln_kernel,
        grid=(t // tm,),
        in_specs=[
            pl.BlockSpec((tm, kdim), row),
            pl.BlockSpec((None, kdim, D_MODEL), lambda i: (layer, 0, 0)),
            pl.BlockSpec((tm, D_MODEL), row),
            pl.BlockSpec((1, D_MODEL), const),
            pl.BlockSpec((1, D_MODEL), const),
        ],
        out_specs=pl.BlockSpec((tm, D_MODEL), row),
        out_shape=jax.ShapeDtypeStruct((t, D_MODEL), F32),
        compiler_params=_compiler_params(("parallel",)),
        name="proj_res_ln",
    )(a, w_stack, x, g, b)


def _mlp_kernel(x_ref, wu_ref, wd_ref, g_ref, b_ref, o_ref, xb_ref):
    j = pl.program_id(1)

    @pl.when(j == 0)
    def _():
        xb_ref[...] = x_ref[...].astype(BF16)
        o_ref[...] = jnp.zeros_like(o_ref)

    h = jnp.dot(xb_ref[...], wu_ref[...], preferred_element_type=F32)
    h = jnp.square(jnp.maximum(h, 0.0)).astype(BF16)
    o_ref[...] += jnp.dot(h, wd_ref[...], preferred_element_type=F32)

    @pl.when(j == pl.num_programs(1) - 1)
    def _():
        o_ref[...] = _layer_norm_rows(ALPHA * x_ref[...] + o_ref[...], g_ref[...], b_ref[...])


def _mlp(x, wu_stack, wd_stack, layer, g, b, tm, tf):
    t = x.shape[0]
    row = lambda i, j: (i, 0)
    const = lambda i, j: (0, 0)
    return pl.pallas_call(
        _mlp_kernel,
        grid=(t // tm, D_FF // tf),
        in_specs=[
            pl.BlockSpec((tm, D_MODEL), row),
            pl.BlockSpec((None, D_MODEL, tf), lambda i, j: (layer, 0, j)),
            pl.BlockSpec((None, tf, D_MODEL), lambda i, j: (layer, j, 0)),
            pl.BlockSpec((1, D_MODEL), const),
            pl.BlockSpec((1, D_MODEL), const),
        ],
        out_specs=pl.BlockSpec((tm, D_MODEL), row),
        out_shape=jax.ShapeDtypeStruct((t, D_MODEL), F32),
        scratch_shapes=[pltpu.VMEM((tm, D_MODEL), BF16)],
        compiler_params=_compiler_params(("parallel", "arbitrary")),
        name="mlp",
    )(x, wu_stack, wd_stack, g, b)


def _mla_proj_kernel(x_ref, wd_ref, gq_ref, gkv_ref, wuq_ref, wuk_ref, wuvt_ref, tab_ref,
                     q_ref, k_ref, vt_ref):
    xb = x_ref[...].astype(BF16)
    tab = tab_ref[...]
    lane = lax.broadcasted_iota(jnp.int32, tab.shape, 1)

    def rope_pair(y):
        t = y * tab
        return jnp.where(lane < ROPE_DIM, t + pltpu.roll(t, ROPE_DIM, 1), 0.0).astype(BF16)

    lat = jnp.dot(xb, wd_ref[...], preferred_element_type=F32)
    cq = lat[:, :Q_LORA_PAD]
    cq = cq * lax.rsqrt(jnp.sum(cq * cq, axis=-1, keepdims=True) * (1.0 / Q_LORA) + RMS_EPS)
    cq = (cq * gq_ref[...]).astype(BF16)
    ckv = lat[:, Q_LORA_PAD:Q_LORA_PAD + KV_LORA]
    ckv = ckv * lax.rsqrt(jnp.mean(ckv * ckv, axis=-1, keepdims=True) + RMS_EPS)
    ckv = (ckv * gkv_ref[...]).astype(BF16)
    kr = rope_pair(lat[:, Q_LORA_PAD + KV_LORA:])

    heads_per_chunk = 2
    qchunk = heads_per_chunk * QK_SLAB
    for c in range(0, B_HEADS * QK_SLAB, qchunk):
        qq = jnp.dot(cq, wuq_ref[:, c:c + qchunk], preferred_element_type=F32)
        for hh in range(heads_per_chunk):
            o = hh * QK_SLAB
            q_ref[:, c + o:c + o + NOPE_DIM] = qq[:, o:o + NOPE_DIM].astype(BF16)
            q_ref[:, c + o + NOPE_DIM:c + o + QK_SLAB] = rope_pair(qq[:, o + NOPE_DIM:o + QK_SLAB])

    kchunk = 4 * NOPE_DIM
    for c in range(0, B_HEADS * NOPE_DIM, kchunk):
        kn = jnp.dot(ckv, wuk_ref[:, c:c + kchunk], preferred_element_type=F32)
        for hh in range(kchunk // NOPE_DIM):
            h = c // NOPE_DIM + hh
            k_ref[:, h * QK_SLAB:h * QK_SLAB + NOPE_DIM] = (
                kn[:, hh * NOPE_DIM:(hh + 1) * NOPE_DIM].astype(BF16))
            k_ref[:, h * QK_SLAB + NOPE_DIM:(h + 1) * QK_SLAB] = kr
    for c in range(0, B_HEADS * V_DIM, kchunk):
        vt_ref[c:c + kchunk, :] = _dot_nt(wuvt_ref[c:c + kchunk, :], ckv).astype(BF16)


def _mla_proj(x, wd, gq, gkv, wuq, wuk, wuvt, tab, seq, tm):
    t = x.shape[0]
    tiles_per_seq = seq // tm
    row = lambda i: (i, 0)
    const = lambda i: (0, 0)
    full = lambda a: pl.BlockSpec(a.shape, const)
    return pl.pallas_call(
        _mla_proj_kernel,
        grid=(t // tm,),
        in_specs=[
            pl.BlockSpec((tm, D_MODEL), row),
            full(wd), full(gq), full(gkv), full(wuq), full(wuk), full(wuvt),
            pl.BlockSpec((tm, LANES), lambda i: (i % tiles_per_seq, 0)),
        ],
        out_specs=[
            pl.BlockSpec((tm, B_HEADS * QK_SLAB), row),
            pl.BlockSpec((tm, B_HEADS * QK_SLAB), row),
            pl.BlockSpec((B_HEADS * V_DIM, tm), lambda i: (0, i)),
        ],
        out_shape=[
            jax.ShapeDtypeStruct((t, B_HEADS * QK_SLAB), BF16),
            jax.ShapeDtypeStruct((t, B_HEADS * QK_SLAB), BF16),
            jax.ShapeDtypeStruct((B_HEADS * V_DIM, t), BF16),
        ],
        compiler_params=_compiler_params(("parallel",)),
        name="mla_proj",
    )(x, wd, gq, gkv, wuq, wuk, wuvt, tab)


F32_SUBLANES = 8
MLA_EXP2_SCALE = math.log2(math.e) / math.sqrt(NOPE_DIM + ROPE_DIM)


def _mla_attn_kernel(q_ref, k_ref, vt_ref, o_ref, s_ref, acc_ref, *, tk1, tk2, unroll):
    seq = k_ref.shape[0]
    tq = q_ref.shape[0]
    sub = F32_SUBLANES
    q = q_ref[...]

    m8 = jnp.full((sub, tq), -jnp.inf, F32)
    for c0 in range(0, seq, tk1):
        s = _dot_nt(k_ref[c0:c0 + tk1, :], q) * MLA_EXP2_SCALE
        s_ref[c0:c0 + tk1, :] = s
        m8 = jnp.maximum(m8, jnp.max(s.reshape(tk1 // sub, sub, tq), axis=0))
    m = jnp.max(m8, axis=0, keepdims=True)

    acc_ref[...] = jnp.zeros_like(acc_ref)

    def body(c, l8):
        start = pl.multiple_of(c * tk2, tk2)
        p = jnp.exp2(s_ref[pl.ds(start, tk2), :] - m)
        acc_ref[...] += jnp.dot(vt_ref[:, pl.ds(start, tk2)], p.astype(BF16),
                                preferred_element_type=F32)
        return l8 + jnp.sum(p.reshape(tk2 // sub, sub, tq), axis=0)

    l8 = lax.fori_loop(0, seq // tk2, body, jnp.zeros((sub, tq), F32), unroll=unroll)
    inv_l = 1.0 / jnp.sum(l8, axis=0, keepdims=True)
    o_ref[...] = (acc_ref[...] * inv_l).T.astype(BF16)


def _mla_attn(q, k, vt, batch, seq, tq, tk1, tk2, unroll):
    t = q.shape[0]
    nq = seq // tq
    return pl.pallas_call(
        functools.partial(_mla_attn_kernel, tk1=tk1, tk2=tk2, unroll=unroll),
        grid=(batch, B_HEADS, nq),
        in_specs=[
            pl.BlockSpec((tq, QK_SLAB), lambda b, h, i: (b * nq + i, h)),
            pl.BlockSpec((seq, QK_SLAB), lambda b, h, i: (b, h)),
            pl.BlockSpec((V_DIM, seq), lambda b, h, i: (h, b)),
        ],
        out_specs=pl.BlockSpec((tq, V_DIM), lambda b, h, i: (b * nq + i, h)),
        out_shape=jax.ShapeDtypeStruct((t, B_HEADS * V_DIM), BF16),
        scratch_shapes=[
            pltpu.VMEM((seq, tq), F32),
            pltpu.VMEM((V_DIM, tq), F32),
        ],
        compiler_params=_compiler_params(("parallel", "parallel", "arbitrary")),
        name="mla_attn",
    )(q, k, vt)


def _rope_angles(seq, dim):
    inv = 1.0 / (ROPE_THETA ** (jnp.arange(0, dim, 2, dtype=F32) / dim))
    ang = jnp.arange(seq, dtype=F32)[:, None] * inv[None, :]
    return jnp.concatenate([ang, ang], axis=-1)


def _rot_half_cols(w):
    half = w.shape[-1] // 2
    return jnp.concatenate([-w[:, half:], w[:, :half]], axis=-1)


def _prep_mla_weights(w_dqkv, q_norm, kv_norm, w_uq, w_ukv):
    wq = w_dqkv[:, :Q_LORA]
    wkv = w_dqkv[:, Q_LORA:Q_LORA + KV_LORA]
    wr = w_dqkv[:, Q_LORA + KV_LORA:]
    zpad = jnp.zeros((D_MODEL, Q_LORA_PAD - Q_LORA), w_dqkv.dtype)
    wd = jnp.concatenate([wq, zpad, wkv, wr, _rot_half_cols(wr)], axis=1).astype(BF16)
    gq = jnp.pad(q_norm, (0, Q_LORA_PAD - Q_LORA)).reshape(1, Q_LORA_PAD)
    gkv = kv_norm.reshape(1, KV_LORA)
    uq = w_uq.reshape(Q_LORA, B_HEADS, NOPE_DIM + ROPE_DIM)
    uq_r = uq[:, :, NOPE_DIM:]
    uq_rot = jnp.concatenate([-uq_r[..., ROPE_DIM // 2:], uq_r[..., :ROPE_DIM // 2]], axis=-1)
    wuq = jnp.concatenate([uq, uq_rot], axis=-1).reshape(Q_LORA, B_HEADS * QK_SLAB)
    wuq = jnp.pad(wuq, ((0, Q_LORA_PAD - Q_LORA), (0, 0))).astype(BF16)
    ukv = w_ukv.reshape(KV_LORA, B_HEADS, NOPE_DIM + V_DIM)
    wuk = ukv[:, :, :NOPE_DIM].reshape(KV_LORA, B_HEADS * NOPE_DIM).astype(BF16)
    wuvt = ukv[:, :, NOPE_DIM:].reshape(KV_LORA, B_HEADS * V_DIM).T.astype(BF16)
    return wd, gq, gkv, wuq, wuk, wuvt


def _tiles(seq):
    tm = 512
    assert seq % tm == 0 and seq % BLK == 0
    tk2 = 512
    unroll = max(1, min(8, seq // tk2 // 2))
    return dict(tm=tm, tf=1024, tq=512, tk1=min(seq, 2048), tk2=tk2, unroll=unroll)


def _trunk(x3, weights):
    batch, seq, _ = x3.shape
    cfg = _tiles(seq)
    tm = cfg["tm"]
    x = x3.reshape(batch * seq, D_MODEL)

    ang_a = _rope_angles(seq, A_HEAD_DIM)
    sign = jnp.where(jnp.arange(A_HEAD_DIM) < A_HEAD_DIM // 2, -1.0, 1.0).astype(F32)
    cos_a, sin_a = jnp.cos(ang_a), jnp.sin(ang_a) * sign
    ang_b = _rope_angles(seq, ROPE_DIM)
    tab_b = jnp.concatenate([jnp.cos(ang_b), jnp.sin(ang_b)], axis=-1)

    for i in range(DEPTH):
        j = i // N_MIXERS
        if i % N_MIXERS == 0:
            q, k, v = _qkv_rope(x, weights["w_qkv_a"], j, cos_a, sin_a, seq, tm)
            attn = _win_attn(q, k, v, weights["sink_a"][j], batch, seq)
            w_o = weights["w_o_a"]
        else:
            q, k, v = _mla_proj(x, *weights["mla"][j], tab_b, seq, tm)
            attn = _mla_attn(q, k, v, batch, seq, cfg["tq"], cfg["tk1"], cfg["tk2"],
                             cfg["unroll"])
            w_o = weights["w_o_b"]
        g, b = weights["ln_g"], weights["ln_b"]
        x = _proj_res_ln(attn, w_o, j, x, g[i, 0:1], b[i, 0:1], tm)
        x = _mlp(x, weights["w_up"], weights["w_down"], i, g[i, 1:2], b[i, 1:2],
                 tm, cfg["tf"])
    return x.reshape(batch, seq, D_MODEL)


def kernel(x_prompt, x_sample, w_qkv_a, sink_a, w_o_a, w_dqkv_b, q_norm_b, kv_norm_b,
           w_uq_b, w_ukv_b, w_o_b, w_up, w_down, ln_g, ln_b):
    weights = dict(
        w_qkv_a=w_qkv_a.astype(BF16),
        sink_a=sink_a.astype(F32),
        w_o_a=w_o_a.astype(BF16),
        mla=[_prep_mla_weights(w_dqkv_b[j], q_norm_b[j], kv_norm_b[j], w_uq_b[j], w_ukv_b[j])
             for j in range(w_dqkv_b.shape[0])],
        w_o_b=w_o_b.astype(BF16),
        w_up=w_up.astype(BF16),
        w_down=w_down.astype(BF16),
        ln_g=ln_g.astype(F32),
        ln_b=ln_b.astype(F32),
    )
    return (_trunk(x_prompt, weights), _trunk(x_sample, weights))
```

```python
import functools
import math

import jax
import jax.numpy as jnp
from jax import lax
from jax.experimental import pallas as pl
from jax.experimental.pallas import tpu as pltpu

D_MODEL = 2048
DEPTH = 4
N_MIXERS = 2
BLK = 128
A_HEADS = 16
A_KV_HEADS = 4
A_GROUP = A_HEADS // A_KV_HEADS
A_HEAD_DIM = D_MODEL // A_HEADS
WINDOW = 128
B_HEADS = 16
Q_LORA = 448
KV_LORA = 128
NOPE_DIM = 128
ROPE_DIM = 64
V_DIM = 128
D_FF = 4 * D_MODEL
ROPE_THETA = 10000.0
LN_EPS = 1e-5
RMS_EPS = 1e-6
ALPHA = (2 * DEPTH) ** 0.25

LANES = 128
MXU_DIM = 256
VMEM_LIMIT_BYTES = 56 * 1024 * 1024

Q_LORA_PAD = 512
LAT_COLS = Q_LORA_PAD + KV_LORA + 2 * ROPE_DIM
QK_SLAB = MXU_DIM
BF16_SUBLANES = 16
VT_SLAB = V_DIM + BF16_SUBLANES

F32 = jnp.float32
BF16 = jnp.bfloat16


def _compiler_params(semantics):
    return pltpu.CompilerParams(dimension_semantics=semantics,
                                vmem_limit_bytes=VMEM_LIMIT_BYTES)


def _layer_norm_rows(y, g, b):
    mu = jnp.mean(y, axis=-1, keepdims=True)
    d = y - mu
    var = jnp.mean(d * d, axis=-1, keepdims=True)
    return d * lax.rsqrt(var + LN_EPS) * g + b


def _dot_nt(a, b):
    return lax.dot_general(a, b, (((1,), (1,)), ((), ())), preferred_element_type=F32)


def _qkv_rope_kernel(x_ref, w_ref, cos_ref, sin_ref, q_ref, k_ref, v_ref):
    xb = x_ref[...].astype(BF16)
    cos = cos_ref[...]
    sin = sin_ref[...]
    chunk = A_GROUP * A_HEAD_DIM
    q_cols = A_HEADS * A_HEAD_DIM
    k_cols = A_KV_HEADS * A_HEAD_DIM

    def rope(t):
        return t * cos + pltpu.roll(t, A_HEAD_DIM // 2, 1) * sin

    for c in range(0, q_cols + k_cols, chunk):
        acc = jnp.dot(xb, w_ref[:, c:c + chunk], preferred_element_type=F32)
        for hh in range(chunk // A_HEAD_DIM):
            y = rope(acc[:, hh * A_HEAD_DIM:(hh + 1) * A_HEAD_DIM]).astype(BF16)
            col = c + hh * A_HEAD_DIM
            if col < q_cols:
                q_ref[:, col:col + A_HEAD_DIM] = y
            else:
                k_ref[:, col - q_cols:col - q_cols + A_HEAD_DIM] = y
    v0 = q_cols + k_cols
    v_ref[...] = jnp.dot(xb, w_ref[:, v0:v0 + k_cols],
                         preferred_element_type=F32).astype(BF16)


def _qkv_rope(x, w_stack, layer, cos, sin, seq, tm):
    t = x.shape[0]
    tiles_per_seq = seq // tm
    q_cols = A_HEADS * A_HEAD_DIM
    k_cols = A_KV_HEADS * A_HEAD_DIM
    row = lambda i: (i, 0)
    pos = lambda i: (i % tiles_per_seq, 0)
    return pl.pallas_call(
        _qkv_rope_kernel,
        grid=(t // tm,),
        in_specs=[
            pl.BlockSpec((tm, D_MODEL), row),
            pl.BlockSpec((None, D_MODEL, q_cols + 2 * k_cols), lambda i: (layer, 0, 0)),
            pl.BlockSpec((tm, A_HEAD_DIM), pos),
            pl.BlockSpec((tm, A_HEAD_DIM), pos),
        ],
        out_specs=[
            pl.BlockSpec((tm, q_cols), row),
            pl.BlockSpec((tm, k_cols), row),
            pl.BlockSpec((tm, k_cols), row),
        ],
        out_shape=[
            jax.ShapeDtypeStruct((t, q_cols), BF16),
            jax.ShapeDtypeStruct((t, k_cols), BF16),
            jax.ShapeDtypeStruct((t, k_cols), BF16),
        ],
        compiler_params=_compiler_params(("parallel",)),
        name="qkv_rope",
    )(x, w_stack, cos, sin)


def _win_attn_kernel(sink_ref, q_ref, kp_ref, kc_ref, kn_ref, vp_ref, vc_ref, vn_ref,
                     o_ref, *, nb):
    n = pl.program_id(1)
    rows = A_GROUP * BLK
    span = BLK + 2 * WINDOW
    qi = lax.broadcasted_iota(jnp.int32, (rows, span), 0) & (BLK - 1)
    col = lax.broadcasted_iota(jnp.int32, (rows, span), 1)
    lo = jnp.where(n > 0, 0, WINDOW)
    hi = jnp.where(n < nb - 1, span, BLK + WINDOW)
    valid = (col >= jnp.maximum(qi, lo)) & (col <= qi + 2 * WINDOW) & (col < hi)
    bias = jnp.where(valid, 0.0, -jnp.inf).astype(F32)
    grp = lax.broadcasted_iota(jnp.int32, (rows, 1), 0) // BLK
    log2e = math.log2(math.e)
    scale2 = log2e / math.sqrt(A_HEAD_DIM)

    for h in range(A_KV_HEADS):
        ks = slice(h * A_HEAD_DIM, (h + 1) * A_HEAD_DIM)
        qh = jnp.concatenate(
            [q_ref[:, (h * A_GROUP + g) * A_HEAD_DIM:(h * A_GROUP + g + 1) * A_HEAD_DIM]
             for g in range(A_GROUP)], axis=0)
        kh = jnp.concatenate([kp_ref[:, ks], kc_ref[:, ks], kn_ref[:, ks]], axis=0)
        vh = jnp.concatenate([vp_ref[:, ks], vc_ref[:, ks], vn_ref[:, ks]], axis=0)
        s = _dot_nt(qh, kh) * scale2 + bias
        sink = jnp.full((rows, 1), sink_ref[h * A_GROUP] * log2e, F32)
        for g in range(1, A_GROUP):
            sink = jnp.where(grp == g, sink_ref[h * A_GROUP + g] * log2e, sink)
        m = jnp.maximum(jnp.max(s, axis=-1, keepdims=True), sink)
        p = jnp.exp2(s - m)
        denom = jnp.sum(p, axis=-1, keepdims=True) + jnp.exp2(sink - m)
        o = jnp.dot(p.astype(BF16), vh, preferred_element_type=F32) * (1.0 / denom)
        for g in range(A_GROUP):
            c0 = (h * A_GROUP + g) * A_HEAD_DIM
            o_ref[:, c0:c0 + A_HEAD_DIM] = o[g * BLK:(g + 1) * BLK].astype(BF16)


def _win_attn(q, k, v, sink, batch, seq):
    t = q.shape[0]
    nb = seq // BLK
    q_cols = A_HEADS * A_HEAD_DIM
    k_cols = A_KV_HEADS * A_HEAD_DIM
    cur = lambda b, n: (b * nb + n, 0)
    prev = lambda b, n: (b * nb + jnp.maximum(n - 1, 0), 0)
    nxt = lambda b, n: (b * nb + jnp.minimum(n + 1, nb - 1), 0)
    kv_spec = lambda im: pl.BlockSpec((BLK, k_cols), im)
    return pl.pallas_call(
        functools.partial(_win_attn_kernel, nb=nb),
        grid=(batch, nb),
        in_specs=[
            pl.BlockSpec(memory_space=pltpu.SMEM),
            pl.BlockSpec((BLK, q_cols), cur),
            kv_spec(prev), kv_spec(cur), kv_spec(nxt),
            kv_spec(prev), kv_spec(cur), kv_spec(nxt),
        ],
        out_specs=pl.BlockSpec((BLK, q_cols), cur),
        out_shape=jax.ShapeDtypeStruct((t, q_cols), BF16),
        compiler_params=_compiler_params(("parallel", "parallel")),
        name="win_attn",
    )(sink, q, k, k, k, v, v, v)


def _proj_res_ln_kernel(a_ref, w_ref, x_ref, g_ref, b_ref, o_ref):
    h = jnp.dot(a_ref[...], w_ref[...], preferred_element_type=F32)
    o_ref[...] = _layer_norm_rows(ALPHA * x_ref[...] + h, g_ref[...], b_ref[...])


def _proj_res_ln(a, w_stack, layer, x, g, b, tm):
    t, kdim = a.shape
    row = lambda i: (i, 0)
    const = lambda i: (0, 0)
    return pl.pallas_call(
        _proj_res_ln_kernel,
        grid=(t // tm,),
        in_specs=[
            pl.BlockSpec((tm, kdim), row),
            pl.BlockSpec((None, kdim, D_MODEL), lambda i: (layer, 0, 0)),
            pl.BlockSpec((tm, D_MODEL), row),
            pl.BlockSpec((1, D_MODEL), const),
            pl.BlockSpec((1, D_MODEL), const),
        ],
        out_specs=pl.BlockSpec((tm, D_MODEL), row),
        out_shape=jax.ShapeDtypeStruct((t, D_MODEL), F32),
        compiler_params=_compiler_params(("parallel",)),
        name="proj_res_ln",
    )(a, w_stack, x, g, b)


def _mlp_kernel(x_ref, wu_ref, wd_ref, g_ref, b_ref, o_ref, xb_ref):
    j = pl.program_id(1)

    @pl.when(j == 0)
    def _():
        xb_ref[...] = x_ref[...].astype(BF16)
        o_ref[...] = jnp.zeros_like(o_ref)

    h = jnp.dot(xb_ref[...], wu_ref[...], preferred_element_type=F32)
    h = jnp.square(jnp.maximum(h, 0.0)).astype(BF16)
    o_ref[...] += jnp.dot(h, wd_ref[...], preferred_element_type=F32)

    @pl.when(j == pl.num_programs(1) - 1)
    def _():
        o_ref[...] = _layer_norm_rows(ALPHA * x_ref[...] + o_ref[...], g_ref[...], b_ref[...])


def _mlp(x, wu_stack, wd_stack, layer, g, b, tm, tf):
    t = x.shape[0]
    row = lambda i, j: (i, 0)
    const = lambda i, j: (0, 0)
    return pl.pallas_call(
        _mlp_kernel,
        grid=(t // tm, D_FF // tf),
        in_specs=[
            pl.BlockSpec((tm, D_MODEL), row),
            pl.BlockSpec((None, D_MODEL, tf), lambda i, j: (layer, 0, j)),
            pl.BlockSpec((None, tf, D_MODEL), lambda i, j: (layer, j, 0)),
            pl.BlockSpec((1, D_MODEL), const),
            pl.BlockSpec((1, D_MODEL), const),
        ],
        out_specs=pl.BlockSpec((tm, D_MODEL), row),
        out_shape=jax.ShapeDtypeStruct((t, D_MODEL), F32),
        scratch_shapes=[pltpu.VMEM((tm, D_MODEL), BF16)],
        compiler_params=_compiler_params(("parallel", "arbitrary")),
        name="mlp",
    )(x, wu_stack, wd_stack, g, b)


def _mla_proj_kernel(x_ref, wd_ref, gq_ref, gkv_ref, wuq_ref, wuk_ref, wuvt_ref, tab_ref,
                     q_ref, k_ref, vt_ref):
    xb = x_ref[...].astype(BF16)
    tab = tab_ref[...]
    lane = lax.broadcasted_iota(jnp.int32, tab.shape, 1)

    def rope_pair(y):
        t = y * tab
        return jnp.where(lane < ROPE_DIM, t + pltpu.roll(t, ROPE_DIM, 1), 0.0).astype(BF16)

    lat = jnp.dot(xb, wd_ref[...], preferred_element_type=F32)
    cq = lat[:, :Q_LORA_PAD]
    cq = cq * lax.rsqrt(jnp.sum(cq * cq, axis=-1, keepdims=True) * (1.0 / Q_LORA) + RMS_EPS)
    cq = (cq * gq_ref[...]).astype(BF16)
    ckv = lat[:, Q_LORA_PAD:Q_LORA_PAD + KV_LORA]
    ckv = ckv * lax.rsqrt(jnp.mean(ckv * ckv, axis=-1, keepdims=True) + RMS_EPS)
    ckv = (ckv * gkv_ref[...]).astype(BF16)
    kr = rope_pair(lat[:, Q_LORA_PAD + KV_LORA:])

    heads_per_chunk = 2
    qchunk = heads_per_chunk * QK_SLAB
    for c in range(0, B_HEADS * QK_SLAB, qchunk):
        qq = jnp.dot(cq, wuq_ref[:, c:c + qchunk], preferred_element_type=F32)
        for hh in range(heads_per_chunk):
            o = hh * QK_SLAB
            q_ref[:, c + o:c + o + NOPE_DIM] = qq[:, o:o + NOPE_DIM].astype(BF16)
            q_ref[:, c + o + NOPE_DIM:c + o + QK_SLAB] = rope_pair(qq[:, o + NOPE_DIM:o + QK_SLAB])

    kchunk = 4 * NOPE_DIM
    for c in range(0, B_HEADS * NOPE_DIM, kchunk):
        kn = jnp.dot(ckv, wuk_ref[:, c:c + kchunk], preferred_element_type=F32)
        for hh in range(kchunk // NOPE_DIM):
            h = c // NOPE_DIM + hh
            k_ref[:, h * QK_SLAB:h * QK_SLAB + NOPE_DIM] = (
                kn[:, hh * NOPE_DIM:(hh + 1) * NOPE_DIM].astype(BF16))
            k_ref[:, h * QK_SLAB + NOPE_DIM:(h + 1) * QK_SLAB] = kr
    ones = jnp.ones((VT_SLAB - V_DIM, ckv.shape[0]), BF16)
    for c in range(0, B_HEADS * V_DIM, kchunk):
        vt = _dot_nt(wuvt_ref[c:c + kchunk, :], ckv).astype(BF16)
        for hh in range(kchunk // V_DIM):
            r0 = (c // V_DIM + hh) * VT_SLAB
            vt_ref[r0:r0 + V_DIM, :] = vt[hh * V_DIM:(hh + 1) * V_DIM]
            vt_ref[r0 + V_DIM:r0 + VT_SLAB, :] = ones


def _mla_proj(x, wd, gq, gkv, wuq, wuk, wuvt, tab, seq, tm):
    t = x.shape[0]
    tiles_per_seq = seq // tm
    row = lambda i: (i, 0)
    const = lambda i: (0, 0)
    full = lambda a: pl.BlockSpec(a.shape, const)
    return pl.pallas_call(
        _mla_proj_kernel,
        grid=(t // tm,),
        in_specs=[
            pl.BlockSpec((tm, D_MODEL), row),
            full(wd), full(gq), full(gkv), full(wuq), full(wuk), full(wuvt),
            pl.BlockSpec((tm, LANES), lambda i: (i % tiles_per_seq, 0)),
        ],
        out_specs=[
            pl.BlockSpec((tm, B_HEADS * QK_SLAB), row),
            pl.BlockSpec((tm, B_HEADS * QK_SLAB), row),
            pl.BlockSpec((B_HEADS * VT_SLAB, tm), lambda i: (0, i)),
        ],
        out_shape=[
            jax.ShapeDtypeStruct((t, B_HEADS * QK_SLAB), BF16),
            jax.ShapeDtypeStruct((t, B_HEADS * QK_SLAB), BF16),
            jax.ShapeDtypeStruct((B_HEADS * VT_SLAB, t), BF16),
        ],
        compiler_params=_compiler_params(("parallel",)),
        name="mla_proj",
    )(x, wd, gq, gkv, wuq, wuk, wuvt, tab)


F32_SUBLANES = 8
MLA_EXP2_SCALE = math.log2(math.e) / math.sqrt(NOPE_DIM + ROPE_DIM)


def _mla_attn_kernel(q_ref, k_ref, vt_ref, o_ref, s_ref, acc_ref, *, tk1, tk2, unroll):
    seq = k_ref.shape[0]
    tq = q_ref.shape[0]
    sub = F32_SUBLANES
    q = q_ref[...]

    m8 = jnp.full((sub, tq), -jnp.inf, F32)
    for c0 in range(0, seq, tk1):
        s = _dot_nt(k_ref[c0:c0 + tk1, :], q) * MLA_EXP2_SCALE
        s_ref[c0:c0 + tk1, :] = s
        m8 = jnp.maximum(m8, jnp.max(s.reshape(tk1 // sub, sub, tq), axis=0))
    m = jnp.max(m8, axis=0, keepdims=True)

    acc_ref[...] = jnp.zeros_like(acc_ref)

    def body(c, carry):
        start = pl.multiple_of(c * tk2, tk2)
        p = jnp.exp2((s_ref[pl.ds(start, tk2), :] - m).astype(BF16))
        acc_ref[...] += jnp.dot(vt_ref[:, pl.ds(start, tk2)], p, preferred_element_type=F32)
        return carry

    lax.fori_loop(0, seq // tk2, body, 0, unroll=unroll)
    inv_l = 1.0 / acc_ref[V_DIM:V_DIM + 1, :]
    o_ref[...] = (acc_ref[:V_DIM, :] * inv_l).T.astype(BF16)


def _mla_attn(q, k, vt, batch, seq, tq, tk1, tk2, unroll):
    t = q.shape[0]
    nq = seq // tq
    return pl.pallas_call(
        functools.partial(_mla_attn_kernel, tk1=tk1, tk2=tk2, unroll=unroll),
        grid=(batch, B_HEADS, nq),
        in_specs=[
            pl.BlockSpec((tq, QK_SLAB), lambda b, h, i: (b * nq + i, h)),
            pl.BlockSpec((seq, QK_SLAB), lambda b, h, i: (b, h)),
            pl.BlockSpec((VT_SLAB, seq), lambda b, h, i: (h, b)),
        ],
        out_specs=pl.BlockSpec((tq, V_DIM), lambda b, h, i: (b * nq + i, h)),
        out_shape=jax.ShapeDtypeStruct((t, B_HEADS * V_DIM), BF16),
        scratch_shapes=[
            pltpu.VMEM((seq, tq), F32),
            pltpu.VMEM((VT_SLAB, tq), F32),
        ],
        compiler_params=_compiler_params(("parallel", "parallel", "arbitrary")),
        name="mla_attn",
    )(q, k, vt)


def _rope_angles(seq, dim):
    inv = 1.0 / (ROPE_THETA ** (jnp.arange(0, dim, 2, dtype=F32) / dim))
    ang = jnp.arange(seq, dtype=F32)[:, None] * inv[None, :]
    return jnp.concatenate([ang, ang], axis=-1)


def _rot_half_cols(w):
    half = w.shape[-1] // 2
    return jnp.concatenate([-w[:, half:], w[:, :half]], axis=-1)


def _prep_mla_weights(w_dqkv, q_norm, kv_norm, w_uq, w_ukv):
    wq = w_dqkv[:, :Q_LORA]
    wkv = w_dqkv[:, Q_LORA:Q_LORA + KV_LORA]
    wr = w_dqkv[:, Q_LORA + KV_LORA:]
    zpad = jnp.zeros((D_MODEL, Q_LORA_PAD - Q_LORA), w_dqkv.dtype)
    wd = jnp.concatenate([wq, zpad, wkv, wr, _rot_half_cols(wr)], axis=1).astype(BF16)
    gq = jnp.pad(q_norm, (0, Q_LORA_PAD - Q_LORA)).reshape(1, Q_LORA_PAD)
    gkv = kv_norm.reshape(1, KV_LORA)
    uq = w_uq.reshape(Q_LORA, B_HEADS, NOPE_DIM + ROPE_DIM)
    uq_r = uq[:, :, NOPE_DIM:]
    uq_rot = jnp.concatenate([-uq_r[..., ROPE_DIM // 2:], uq_r[..., :ROPE_DIM // 2]], axis=-1)
    wuq = jnp.concatenate([uq, uq_rot], axis=-1).reshape(Q_LORA, B_HEADS * QK_SLAB)
    wuq = jnp.pad(wuq, ((0, Q_LORA_PAD - Q_LORA), (0, 0))).astype(BF16)
    ukv = w_ukv.reshape(KV_LORA, B_HEADS, NOPE_DIM + V_DIM)
    wuk = ukv[:, :, :NOPE_DIM].reshape(KV_LORA, B_HEADS * NOPE_DIM).astype(BF16)
    wuvt = ukv[:, :, NOPE_DIM:].reshape(KV_LORA, B_HEADS * V_DIM).T.astype(BF16)
    return wd, gq, gkv, wuq, wuk, wuvt


def _tiles(seq):
    tm = 512
    assert seq % tm == 0 and seq % BLK == 0
    tk2 = 512
    unroll = max(1, min(8, seq // tk2 // 2))
    return dict(tm=tm, tf=1024, tq=512, tk1=min(seq, 2048), tk2=tk2, unroll=unroll)


def _trunk(x3, weights):
    batch, seq, _ = x3.shape
    cfg = _tiles(seq)
    tm = cfg["tm"]
    x = x3.reshape(batch * seq, D_MODEL)

    ang_a = _rope_angles(seq, A_HEAD_DIM)
    sign = jnp.where(jnp.arange(A_HEAD_DIM) < A_HEAD_DIM // 2, -1.0, 1.0).astype(F32)
    cos_a, sin_a = jnp.cos(ang_a), jnp.sin(ang_a) * sign
    ang_b = _rope_angles(seq, ROPE_DIM)
    tab_b = jnp.concatenate([jnp.cos(ang_b), jnp.sin(ang_b)], axis=-1)

    for i in range(DEPTH):
        j = i // N_MIXERS
        if i % N_MIXERS == 0:
            q, k, v = _qkv_rope(x, weights["w_qkv_a"], j, cos_a, sin_a, seq, tm)
            attn = _win_attn(q, k, v, weights["sink_a"][j], batch, seq)
            w_o = weights["w_o_a"]
        else:
            q, k, v = _mla_proj(x, *weights["mla"][j], tab_b, seq, tm)
            attn = _mla_attn(q, k, v, batch, seq, cfg["tq"], cfg["tk1"], cfg["tk2"],
                             cfg["unroll"])
            w_o = weights["w_o_b"]
        g, b = weights["ln_g"], weights["ln_b"]
        x = _proj_res_ln(attn, w_o, j, x, g[i, 0:1], b[i, 0:1], tm)
        x = _mlp(x, weights["w_up"], weights["w_down"], i, g[i, 1:2], b[i, 1:2],
                 tm, cfg["tf"])
    return x.reshape(batch, seq, D_MODEL)


def kernel(x_prompt, x_sample, w_qkv_a, sink_a, w_o_a, w_dqkv_b, q_norm_b, kv_norm_b,
           w_uq_b, w_ukv_b, w_o_b, w_up, w_down, ln_g, ln_b):
    weights = dict(
        w_qkv_a=w_qkv_a.astype(BF16),
        sink_a=sink_a.astype(F32),
        w_o_a=w_o_a.astype(BF16),
        mla=[_prep_mla_weights(w_dqkv_b[j], q_norm_b[j], kv_norm_b[j], w_uq_b[j], w_ukv_b[j])
             for j in range(w_dqkv_b.shape[0])],
        w_o_b=w_o_b.astype(BF16),
        w_up=w_up.astype(BF16),
        w_down=w_down.astype(BF16),
        ln_g=ln_g.astype(F32),
        ln_b=ln_b.astype(F32),
    )
    return (_trunk(x_prompt, weights), _trunk(x_sample, weights))
```

```python
import functools
import math

import jax
import jax.numpy as jnp
from jax import lax
from jax.experimental import pallas as pl
from jax.experimental.pallas import tpu as pltpu

D_MODEL = 2048
DEPTH = 4
N_MIXERS = 2
BLK = 128
A_HEADS = 16
A_KV_HEADS = 4
A_GROUP = A_HEADS // A_KV_HEADS
A_HEAD_DIM = D_MODEL // A_HEADS
WINDOW = 128
B_HEADS = 16
Q_LORA = 448
KV_LORA = 128
NOPE_DIM = 128
ROPE_DIM = 64
V_DIM = 128
D_FF = 4 * D_MODEL
ROPE_THETA = 10000.0
LN_EPS = 1e-5
RMS_EPS = 1e-6
ALPHA = (2 * DEPTH) ** 0.25

LANES = 128
MXU_DIM = 256
VMEM_LIMIT_BYTES = 56 * 1024 * 1024

Q_LORA_PAD = 512
LAT_COLS = Q_LORA_PAD + KV_LORA + 2 * ROPE_DIM
QK_SLAB = MXU_DIM
BF16_SUBLANES = 16
VT_SLAB = V_DIM + BF16_SUBLANES

F32 = jnp.float32
BF16 = jnp.bfloat16


def _compiler_params(semantics):
    return pltpu.CompilerParams(dimension_semantics=semantics,
                                vmem_limit_bytes=VMEM_LIMIT_BYTES)


def _layer_norm_rows(y, g, b):
    mu = jnp.mean(y, axis=-1, keepdims=True)
    d = y - mu
    var = jnp.mean(d * d, axis=-1, keepdims=True)
    return d * lax.rsqrt(var + LN_EPS) * g + b


def _dot_nt(a, b):
    return lax.dot_general(a, b, (((1,), (1,)), ((), ())), preferred_element_type=F32)


def _qkv_rope_kernel(x_ref, w_ref, cos_ref, sin_ref, q_ref, k_ref, v_ref):
    xb = x_ref[...].astype(BF16)
    cos = cos_ref[...]
    sin = sin_ref[...]
    chunk = A_GROUP * A_HEAD_DIM
    q_cols = A_HEADS * A_HEAD_DIM
    k_cols = A_KV_HEADS * A_HEAD_DIM

    def rope(t):
        return t * cos + pltpu.roll(t, A_HEAD_DIM // 2, 1) * sin

    for c in range(0, q_cols + k_cols, chunk):
        acc = jnp.dot(xb, w_ref[:, c:c + chunk], preferred_element_type=F32)
        for hh in range(chunk // A_HEAD_DIM):
            y = rope(acc[:, hh * A_HEAD_DIM:(hh + 1) * A_HEAD_DIM]).astype(BF16)
            col = c + hh * A_HEAD_DIM
            if col < q_cols:
                q_ref[:, col:col + A_HEAD_DIM] = y
            else:
                k_ref[:, col - q_cols:col - q_cols + A_HEAD_DIM] = y
    v0 = q_cols + k_cols
    v_ref[...] = jnp.dot(xb, w_ref[:, v0:v0 + k_cols],
                         preferred_element_type=F32).astype(BF16)


def _qkv_rope(x, w_stack, layer, cos, sin, seq, tm):
    t = x.shape[0]
    tiles_per_seq = seq // tm
    q_cols = A_HEADS * A_HEAD_DIM
    k_cols = A_KV_HEADS * A_HEAD_DIM
    row = lambda i: (i, 0)
    pos = lambda i: (i % tiles_per_seq, 0)
    return pl.pallas_call(
        _qkv_rope_kernel,
        grid=(t // tm,),
        in_specs=[
            pl.BlockSpec((tm, D_MODEL), row),
            pl.BlockSpec((None, D_MODEL, q_cols + 2 * k_cols), lambda i: (layer, 0, 0)),
            pl.BlockSpec((tm, A_HEAD_DIM), pos),
            pl.BlockSpec((tm, A_HEAD_DIM), pos),
        ],
        out_specs=[
            pl.BlockSpec((tm, q_cols), row),
            pl.BlockSpec((tm, k_cols), row),
            pl.BlockSpec((tm, k_cols), row),
        ],
        out_shape=[
            jax.ShapeDtypeStruct((t, q_cols), BF16),
            jax.ShapeDtypeStruct((t, k_cols), BF16),
            jax.ShapeDtypeStruct((t, k_cols), BF16),
        ],
        compiler_params=_compiler_params(("parallel",)),
        name="qkv_rope",
    )(x, w_stack, cos, sin)


def _win_attn_kernel(sink_ref, q_ref, kp_ref, kc_ref, kn_ref, vp_ref, vc_ref, vn_ref,
                     o_ref, *, nb):
    n = pl.program_id(1)
    rows = A_GROUP * BLK
    span = BLK + 2 * WINDOW
    qi = lax.broadcasted_iota(jnp.int32, (rows, span), 0) & (BLK - 1)
    col = lax.broadcasted_iota(jnp.int32, (rows, span), 1)
    lo = jnp.where(n > 0, 0, WINDOW)
    hi = jnp.where(n < nb - 1, span, BLK + WINDOW)
    valid = (col >= jnp.maximum(qi, lo)) & (col <= qi + 2 * WINDOW) & (col < hi)
    bias = jnp.where(valid, 0.0, -jnp.inf).astype(F32)
    grp = lax.broadcasted_iota(jnp.int32, (rows, 1), 0) // BLK
    log2e = math.log2(math.e)
    scale2 = log2e / math.sqrt(A_HEAD_DIM)

    for h in range(A_KV_HEADS):
        ks = slice(h * A_HEAD_DIM, (h + 1) * A_HEAD_DIM)
        qh = jnp.concatenate(
            [q_ref[:, (h * A_GROUP + g) * A_HEAD_DIM:(h * A_GROUP + g + 1) * A_HEAD_DIM]
             for g in range(A_GROUP)], axis=0)
        kh = jnp.concatenate([kp_ref[:, ks], kc_ref[:, ks], kn_ref[:, ks]], axis=0)
        vh = jnp.concatenate([vp_ref[:, ks], vc_ref[:, ks], vn_ref[:, ks]], axis=0)
        s = _dot_nt(qh, kh) * scale2 + bias
        sink = jnp.full((rows, 1), sink_ref[h * A_GROUP] * log2e, F32)
        for g in range(1, A_GROUP):
            sink = jnp.where(grp == g, sink_ref[h * A_GROUP + g] * log2e, sink)
        m = jnp.maximum(jnp.max(s, axis=-1, keepdims=True), sink)
        p = jnp.exp2(s - m)
        denom = jnp.sum(p, axis=-1, keepdims=True) + jnp.exp2(sink - m)
        o = jnp.dot(p.astype(BF16), vh, preferred_element_type=F32) * (1.0 / denom)
        for g in range(A_GROUP):
            c0 = (h * A_GROUP + g) * A_HEAD_DIM
            o_ref[:, c0:c0 + A_HEAD_DIM] = o[g * BLK:(g + 1) * BLK].astype(BF16)


def _win_attn(q, k, v, sink, batch, seq):
    t = q.shape[0]
    nb = seq // BLK
    q_cols = A_HEADS * A_HEAD_DIM
    k_cols = A_KV_HEADS * A_HEAD_DIM
    cur = lambda b, n: (b * nb + n, 0)
    prev = lambda b, n: (b * nb + jnp.maximum(n - 1, 0), 0)
    nxt = lambda b, n: (b * nb + jnp.minimum(n + 1, nb - 1), 0)
    kv_spec = lambda im: pl.BlockSpec((BLK, k_cols), im)
    return pl.pallas_call(
        functools.partial(_win_attn_kernel, nb=nb),
        grid=(batch, nb),
        in_specs=[
            pl.BlockSpec(memory_space=pltpu.SMEM),
            pl.BlockSpec((BLK, q_cols), cur),
            kv_spec(prev), kv_spec(cur), kv_spec(nxt),
            kv_spec(prev), kv_spec(cur), kv_spec(nxt),
        ],
        out_specs=pl.BlockSpec((BLK, q_cols), cur),
        out_shape=jax.ShapeDtypeStruct((t, q_cols), BF16),
        compiler_params=_compiler_params(("parallel", "parallel")),
        name="win_attn",
    )(sink, q, k, k, k, v, v, v)


def _proj_res_ln_kernel(a_ref, w_ref, x_ref, g_ref, b_ref, o_ref, *, rows):
    for r in range(0, a_ref.shape[0], rows):
        h = jnp.dot(a_ref[r:r + rows, :], w_ref[...], preferred_element_type=F32)
        o_ref[r:r + rows, :] = _layer_norm_rows(ALPHA * x_ref[r:r + rows, :] + h,
                                                g_ref[...], b_ref[...])


def _proj_res_ln(a, w_stack, layer, x, g, b, tm):
    t, kdim = a.shape
    row = lambda i: (i, 0)
    const = lambda i: (0, 0)
    return pl.pallas_call(
        functools.partial(_proj_res_ln_kernel, rows=LANES),
        grid=(t // tm,),
        in_specs=[
            pl.BlockSpec((tm, kdim), row),
            pl.BlockSpec((None, kdim, D_MODEL), lambda i: (layer, 0, 0),
                         pipeline_mode=pl.Buffered(1)),
            pl.BlockSpec((tm, D_MODEL), row),
            pl.BlockSpec((1, D_MODEL), const),
            pl.BlockSpec((1, D_MODEL), const),
        ],
        out_specs=pl.BlockSpec((tm, D_MODEL), row),
        out_shape=jax.ShapeDtypeStruct((t, D_MODEL), F32),
        compiler_params=_compiler_params(("parallel",)),
        name="proj_res_ln",
    )(a, w_stack, x, g, b)


def _mlp_kernel(x_ref, wu_ref, wd_ref, g_ref, b_ref, o_ref, xb_ref, h_ref, *, rows):
    j = pl.program_id(1)

    last = pl.num_programs(1) - 1

    @pl.when(j == 0)
    def _():
        xb_ref[...] = x_ref[...].astype(BF16)

    h = jnp.dot(xb_ref[...], wu_ref[...], preferred_element_type=F32)
    h_ref[...] = jnp.square(jnp.maximum(h, 0.0)).astype(BF16)

    @pl.when(j == 0)
    def _():
        o_ref[...] = jnp.dot(h_ref[...], wd_ref[...], preferred_element_type=F32)

    @pl.when(jnp.logical_and(j > 0, j < last))
    def _():
        o_ref[...] += jnp.dot(h_ref[...], wd_ref[...], preferred_element_type=F32)

    @pl.when(j == last)
    def _():
        for r in range(0, o_ref.shape[0], rows):
            part = jnp.dot(h_ref[r:r + rows, :], wd_ref[...], preferred_element_type=F32)
            y = ALPHA * x_ref[r:r + rows, :] + (o_ref[r:r + rows, :] + part)
            o_ref[r:r + rows, :] = _layer_norm_rows(y, g_ref[...], b_ref[...])


def _mlp(x, wu_stack, wd_stack, layer, g, b, tm, tf):
    t = x.shape[0]
    row = lambda i, j: (i, 0)
    const = lambda i, j: (0, 0)
    return pl.pallas_call(
        functools.partial(_mlp_kernel, rows=LANES),
        grid=(t // tm, D_FF // tf),
        in_specs=[
            pl.BlockSpec((tm, D_MODEL), row),
            pl.BlockSpec((None, D_MODEL, tf), lambda i, j: (layer, 0, j)),
            pl.BlockSpec((None, tf, D_MODEL), lambda i, j: (layer, j, 0)),
            pl.BlockSpec((1, D_MODEL), const),
            pl.BlockSpec((1, D_MODEL), const),
        ],
        out_specs=pl.BlockSpec((tm, D_MODEL), row),
        out_shape=jax.ShapeDtypeStruct((t, D_MODEL), F32),
        scratch_shapes=[
            pltpu.VMEM((tm, D_MODEL), BF16),
            pltpu.VMEM((tm, tf), BF16),
        ],
        compiler_params=_compiler_params(("parallel", "arbitrary")),
        name="mlp",
    )(x, wu_stack, wd_stack, g, b)


def _mla_proj_kernel(x_ref, wd_ref, gq_ref, gkv_ref, wuq_ref, wuk_ref, wuvt_ref, tab_ref,
                     q_ref, k_ref, vt_ref):
    xb = x_ref[...].astype(BF16)
    tab = tab_ref[...]
    lane = lax.broadcasted_iota(jnp.int32, tab.shape, 1)

    def rope_pair(y):
        t = y * tab
        return jnp.where(lane < ROPE_DIM, t + pltpu.roll(t, ROPE_DIM, 1), 0.0).astype(BF16)

    lat = jnp.dot(xb, wd_ref[...], preferred_element_type=F32)
    cq = lat[:, :Q_LORA_PAD]
    cq = cq * lax.rsqrt(jnp.sum(cq * cq, axis=-1, keepdims=True) * (1.0 / Q_LORA) + RMS_EPS)
    cq = (cq * gq_ref[...]).astype(BF16)
    ckv = lat[:, Q_LORA_PAD:Q_LORA_PAD + KV_LORA]
    ckv = ckv * lax.rsqrt(jnp.mean(ckv * ckv, axis=-1, keepdims=True) + RMS_EPS)
    ckv = (ckv * gkv_ref[...]).astype(BF16)
    kr = rope_pair(lat[:, Q_LORA_PAD + KV_LORA:])

    heads_per_chunk = 2
    qchunk = heads_per_chunk * QK_SLAB
    for c in range(0, B_HEADS * QK_SLAB, qchunk):
        qq = jnp.dot(cq, wuq_ref[:, c:c + qchunk], preferred_element_type=F32)
        for hh in range(heads_per_chunk):
            o = hh * QK_SLAB
            q_ref[:, c + o:c + o + NOPE_DIM] = qq[:, o:o + NOPE_DIM].astype(BF16)
            q_ref[:, c + o + NOPE_DIM:c + o + QK_SLAB] = rope_pair(qq[:, o + NOPE_DIM:o + QK_SLAB])

    kchunk = 4 * NOPE_DIM
    for c in range(0, B_HEADS * NOPE_DIM, kchunk):
        kn = jnp.dot(ckv, wuk_ref[:, c:c + kchunk], preferred_element_type=F32)
        for hh in range(kchunk // NOPE_DIM):
            h = c // NOPE_DIM + hh
            k_ref[:, h * QK_SLAB:h * QK_SLAB + NOPE_DIM] = (
                kn[:, hh * NOPE_DIM:(hh + 1) * NOPE_DIM].astype(BF16))
            k_ref[:, h * QK_SLAB + NOPE_DIM:(h + 1) * QK_SLAB] = kr
    ones = jnp.ones((VT_SLAB - V_DIM, ckv.shape[0]), BF16)
    for c in range(0, B_HEADS * V_DIM, kchunk):
        vt = _dot_nt(wuvt_ref[c:c + kchunk, :], ckv).astype(BF16)
        for hh in range(kchunk // V_DIM):
            r0 = (c // V_DIM + hh) * VT_SLAB
            vt_ref[r0:r0 + V_DIM, :] = vt[hh * V_DIM:(hh + 1) * V_DIM]
            vt_ref[r0 + V_DIM:r0 + VT_SLAB, :] = ones


def _mla_proj(x, wd, gq, gkv, wuq, wuk, wuvt, tab, seq, tm):
    t = x.shape[0]
    tiles_per_seq = seq // tm
    row = lambda i: (i, 0)
    const = lambda i: (0, 0)
    full = lambda a: pl.BlockSpec(a.shape, const)
    return pl.pallas_call(
        _mla_proj_kernel,
        grid=(t // tm,),
        in_specs=[
            pl.BlockSpec((tm, D_MODEL), row),
            full(wd), full(gq), full(gkv), full(wuq), full(wuk), full(wuvt),
            pl.BlockSpec((tm, LANES), lambda i: (i % tiles_per_seq, 0)),
        ],
        out_specs=[
            pl.BlockSpec((tm, B_HEADS * QK_SLAB), row),
            pl.BlockSpec((tm, B_HEADS * QK_SLAB), row),
            pl.BlockSpec((B_HEADS * VT_SLAB, tm), lambda i: (0, i)),
        ],
        out_shape=[
            jax.ShapeDtypeStruct((t, B_HEADS * QK_SLAB), BF16),
            jax.ShapeDtypeStruct((t, B_HEADS * QK_SLAB), BF16),
            jax.ShapeDtypeStruct((B_HEADS * VT_SLAB, t), BF16),
        ],
        compiler_params=_compiler_params(("parallel",)),
        name="mla_proj",
    )(x, wd, gq, gkv, wuq, wuk, wuvt, tab)


F32_SUBLANES = 8
MLA_EXP2_SCALE = math.log2(math.e) / math.sqrt(NOPE_DIM + ROPE_DIM)


def _mla_attn_kernel(q_ref, k_ref, vt_ref, o_ref, s_ref, acc_ref, *, tk1, tk2, unroll):
    seq = k_ref.shape[0]
    tq = q_ref.shape[0]
    sub = F32_SUBLANES
    q = q_ref[...]

    m8 = jnp.full((sub, tq), -jnp.inf, F32)
    for c0 in range(0, seq, tk1):
        s = _dot_nt(k_ref[c0:c0 + tk1, :], q) * MLA_EXP2_SCALE
        s_ref[c0:c0 + tk1, :] = s
        m8 = jnp.maximum(m8, jnp.max(s.reshape(tk1 // sub, sub, tq), axis=0))
    m = jnp.max(m8, axis=0, keepdims=True)

    acc_ref[...] = jnp.zeros_like(acc_ref)

    def body(c, carry):
        start = pl.multiple_of(c * tk2, tk2)
        p = jnp.exp2((s_ref[pl.ds(start, tk2), :] - m).astype(BF16))
        acc_ref[...] += jnp.dot(vt_ref[:, pl.ds(start, tk2)], p, preferred_element_type=F32)
        return carry

    lax.fori_loop(0, seq // tk2, body, 0, unroll=unroll)
    inv_l = 1.0 / acc_ref[V_DIM:V_DIM + 1, :]
    o_ref[...] = (acc_ref[:V_DIM, :] * inv_l).T.astype(BF16)


def _mla_attn(q, k, vt, batch, seq, tq, tk1, tk2, unroll):
    t = q.shape[0]
    nq = seq // tq
    return pl.pallas_call(
        functools.partial(_mla_attn_kernel, tk1=tk1, tk2=tk2, unroll=unroll),
        grid=(batch, B_HEADS, nq),
        in_specs=[
            pl.BlockSpec((tq, QK_SLAB), lambda b, h, i: (b * nq + i, h)),
            pl.BlockSpec((seq, QK_SLAB), lambda b, h, i: (b, h)),
            pl.BlockSpec((VT_SLAB, seq), lambda b, h, i: (h, b)),
        ],
        out_specs=pl.BlockSpec((tq, V_DIM), lambda b, h, i: (b * nq + i, h)),
        out_shape=jax.ShapeDtypeStruct((t, B_HEADS * V_DIM), BF16),
        scratch_shapes=[
            pltpu.VMEM((seq, tq), F32),
            pltpu.VMEM((VT_SLAB, tq), F32),
        ],
        compiler_params=_compiler_params(("parallel", "parallel", "arbitrary")),
        name="mla_attn",
    )(q, k, vt)


def _rope_angles(seq, dim):
    inv = 1.0 / (ROPE_THETA ** (jnp.arange(0, dim, 2, dtype=F32) / dim))
    ang = jnp.arange(seq, dtype=F32)[:, None] * inv[None, :]
    return jnp.concatenate([ang, ang], axis=-1)


def _rot_half_cols(w):
    half = w.shape[-1] // 2
    return jnp.concatenate([-w[:, half:], w[:, :half]], axis=-1)


def _prep_mla_weights(w_dqkv, q_norm, kv_norm, w_uq, w_ukv):
    wq = w_dqkv[:, :Q_LORA]
    wkv = w_dqkv[:, Q_LORA:Q_LORA + KV_LORA]
    wr = w_dqkv[:, Q_LORA + KV_LORA:]
    zpad = jnp.zeros((D_MODEL, Q_LORA_PAD - Q_LORA), w_dqkv.dtype)
    wd = jnp.concatenate([wq, zpad, wkv, wr, _rot_half_cols(wr)], axis=1).astype(BF16)
    gq = jnp.pad(q_norm, (0, Q_LORA_PAD - Q_LORA)).reshape(1, Q_LORA_PAD)
    gkv = kv_norm.reshape(1, KV_LORA)
    uq = w_uq.reshape(Q_LORA, B_HEADS, NOPE_DIM + ROPE_DIM)
    uq_r = uq[:, :, NOPE_DIM:]
    uq_rot = jnp.concatenate([-uq_r[..., ROPE_DIM // 2:], uq_r[..., :ROPE_DIM // 2]], axis=-1)
    wuq = jnp.concatenate([uq, uq_rot], axis=-1).reshape(Q_LORA, B_HEADS * QK_SLAB)
    wuq = jnp.pad(wuq, ((0, Q_LORA_PAD - Q_LORA), (0, 0))).astype(BF16)
    ukv = w_ukv.reshape(KV_LORA, B_HEADS, NOPE_DIM + V_DIM)
    wuk = ukv[:, :, :NOPE_DIM].reshape(KV_LORA, B_HEADS * NOPE_DIM).astype(BF16)
    wuvt = ukv[:, :, NOPE_DIM:].reshape(KV_LORA, B_HEADS * V_DIM).T.astype(BF16)
    return wd, gq, gkv, wuq, wuk, wuvt


def _tiles(seq):
    tm = 512
    assert seq % tm == 0 and seq % BLK == 0
    tk2 = 512
    unroll = max(1, min(8, seq // tk2 // 2))
    return dict(tm=tm, tm_proj=1024, tf=1024, tq=1024, tk1=min(seq, 2048), tk2=tk2,
                unroll=unroll)


def _trunk(x3, weights):
    batch, seq, _ = x3.shape
    cfg = _tiles(seq)
    tm = cfg["tm"]
    x = x3.reshape(batch * seq, D_MODEL)

    ang_a = _rope_angles(seq, A_HEAD_DIM)
    sign = jnp.where(jnp.arange(A_HEAD_DIM) < A_HEAD_DIM // 2, -1.0, 1.0).astype(F32)
    cos_a, sin_a = jnp.cos(ang_a), jnp.sin(ang_a) * sign
    ang_b = _rope_angles(seq, ROPE_DIM)
    tab_b = jnp.concatenate([jnp.cos(ang_b), jnp.sin(ang_b)], axis=-1)

    for i in range(DEPTH):
        j = i // N_MIXERS
        if i % N_MIXERS == 0:
            q, k, v = _qkv_rope(x, weights["w_qkv_a"], j, cos_a, sin_a, seq, tm)
            attn = _win_attn(q, k, v, weights["sink_a"][j], batch, seq)
            w_o = weights["w_o_a"]
        else:
            q, k, v = _mla_proj(x, *weights["mla"][j], tab_b, seq, tm)
            attn = _mla_attn(q, k, v, batch, seq, cfg["tq"], cfg["tk1"], cfg["tk2"],
                             cfg["unroll"])
            w_o = weights["w_o_b"]
        g, b = weights["ln_g"], weights["ln_b"]
        x = _proj_res_ln(attn, w_o, j, x, g[i, 0:1], b[i, 0:1], cfg["tm_proj"])
        x = _mlp(x, weights["w_up"], weights["w_down"], i, g[i, 1:2], b[i, 1:2],
                 tm, cfg["tf"])
    return x.reshape(batch, seq, D_MODEL)


def kernel(x_prompt, x_sample, w_qkv_a, sink_a, w_o_a, w_dqkv_b, q_norm_b, kv_norm_b,
           w_uq_b, w_ukv_b, w_o_b, w_up, w_down, ln_g, ln_b):
    weights = dict(
        w_qkv_a=w_qkv_a.astype(BF16),
        sink_a=sink_a.astype(F32),
        w_o_a=w_o_a.astype(BF16),
        mla=[_prep_mla_weights(w_dqkv_b[j], q_norm_b[j], kv_norm_b[j], w_uq_b[j], w_ukv_b[j])
             for j in range(w_dqkv_b.shape[0])],
        w_o_b=w_o_b.astype(BF16),
        w_up=w_up.astype(BF16),
        w_down=w_down.astype(BF16),
        ln_g=ln_g.astype(F32),
        ln_b=ln_b.astype(F32),
    )
    return (_trunk(x_prompt, weights), _trunk(x_sample, weights))
```

```python
import functools
import math

import jax
import jax.numpy as jnp
from jax import lax
from jax.experimental import pallas as pl
from jax.experimental.pallas import tpu as pltpu

D_MODEL = 2048
DEPTH = 4
N_MIXERS = 2
BLK = 128
A_HEADS = 16
A_KV_HEADS = 4
A_GROUP = A_HEADS // A_KV_HEADS
A_HEAD_DIM = D_MODEL // A_HEADS
WINDOW = 128
B_HEADS = 16
Q_LORA = 448
KV_LORA = 128
NOPE_DIM = 128
ROPE_DIM = 64
V_DIM = 128
D_FF = 4 * D_MODEL
ROPE_THETA = 10000.0
LN_EPS = 1e-5
RMS_EPS = 1e-6
ALPHA = (2 * DEPTH) ** 0.25

LANES = 128
MXU_DIM = 256
VMEM_LIMIT_BYTES = 56 * 1024 * 1024

Q_LORA_PAD = 512
LAT_COLS = Q_LORA_PAD + KV_LORA + 2 * ROPE_DIM
QK_SLAB = MXU_DIM
BF16_SUBLANES = 16
VT_SLAB = V_DIM + BF16_SUBLANES

F32 = jnp.float32
BF16 = jnp.bfloat16


def _compiler_params(semantics):
    return pltpu.CompilerParams(dimension_semantics=semantics,
                                vmem_limit_bytes=VMEM_LIMIT_BYTES)


def _layer_norm_rows(y, g, b):
    mu = jnp.mean(y, axis=-1, keepdims=True)
    d = y - mu
    var = jnp.mean(d * d, axis=-1, keepdims=True)
    return d * lax.rsqrt(var + LN_EPS) * g + b


def _dot_nt(a, b):
    return lax.dot_general(a, b, (((1,), (1,)), ((), ())), preferred_element_type=F32)


def _qkv_rope_kernel(x_ref, w_ref, cos_ref, sin_ref, q_ref, k_ref, v_ref):
    xb = x_ref[...].astype(BF16)
    cos = cos_ref[...]
    sin = sin_ref[...]
    chunk = A_GROUP * A_HEAD_DIM
    q_cols = A_HEADS * A_HEAD_DIM
    k_cols = A_KV_HEADS * A_HEAD_DIM

    def rope(t):
        return t * cos + pltpu.roll(t, A_HEAD_DIM // 2, 1) * sin

    for c in range(0, q_cols + k_cols, chunk):
        acc = jnp.dot(xb, w_ref[:, c:c + chunk], preferred_element_type=F32)
        for hh in range(chunk // A_HEAD_DIM):
            y = rope(acc[:, hh * A_HEAD_DIM:(hh + 1) * A_HEAD_DIM]).astype(BF16)
            col = c + hh * A_HEAD_DIM
            if col < q_cols:
                q_ref[:, col:col + A_HEAD_DIM] = y
            else:
                k_ref[:, col - q_cols:col - q_cols + A_HEAD_DIM] = y
    v0 = q_cols + k_cols
    v_ref[...] = jnp.dot(xb, w_ref[:, v0:v0 + k_cols],
                         preferred_element_type=F32).astype(BF16)


def _qkv_rope(x, w_stack, layer, cos, sin, seq, tm):
    t = x.shape[0]
    tiles_per_seq = seq // tm
    q_cols = A_HEADS * A_HEAD_DIM
    k_cols = A_KV_HEADS * A_HEAD_DIM
    row = lambda i: (i, 0)
    pos = lambda i: (i % tiles_per_seq, 0)
    return pl.pallas_call(
        _qkv_rope_kernel,
        grid=(t // tm,),
        in_specs=[
            pl.BlockSpec((tm, D_MODEL), row),
            pl.BlockSpec((None, D_MODEL, q_cols + 2 * k_cols), lambda i: (layer, 0, 0)),
            pl.BlockSpec((tm, A_HEAD_DIM), pos),
            pl.BlockSpec((tm, A_HEAD_DIM), pos),
        ],
        out_specs=[
            pl.BlockSpec((tm, q_cols), row),
            pl.BlockSpec((tm, k_cols), row),
            pl.BlockSpec((tm, k_cols), row),
        ],
        out_shape=[
            jax.ShapeDtypeStruct((t, q_cols), BF16),
            jax.ShapeDtypeStruct((t, k_cols), BF16),
            jax.ShapeDtypeStruct((t, k_cols), BF16),
        ],
        compiler_params=_compiler_params(("parallel",)),
        name="qkv_rope",
    )(x, w_stack, cos, sin)


def _win_attn_kernel(sink_ref, q_ref, kp_ref, kc_ref, kn_ref, vp_ref, vc_ref, vn_ref,
                     o_ref, *, nb):
    n = pl.program_id(1)
    rows = A_GROUP * BLK
    span = BLK + 2 * WINDOW
    qi = lax.broadcasted_iota(jnp.int32, (rows, span), 0) & (BLK - 1)
    col = lax.broadcasted_iota(jnp.int32, (rows, span), 1)
    lo = jnp.where(n > 0, 0, WINDOW)
    hi = jnp.where(n < nb - 1, span, BLK + WINDOW)
    valid = (col >= jnp.maximum(qi, lo)) & (col <= qi + 2 * WINDOW) & (col < hi)
    bias = jnp.where(valid, 0.0, -jnp.inf).astype(F32)
    grp = lax.broadcasted_iota(jnp.int32, (rows, 1), 0) // BLK
    log2e = math.log2(math.e)
    scale2 = log2e / math.sqrt(A_HEAD_DIM)

    for h in range(A_KV_HEADS):
        ks = slice(h * A_HEAD_DIM, (h + 1) * A_HEAD_DIM)
        qh = jnp.concatenate(
            [q_ref[:, (h * A_GROUP + g) * A_HEAD_DIM:(h * A_GROUP + g + 1) * A_HEAD_DIM]
             for g in range(A_GROUP)], axis=0)
        kh = jnp.concatenate([kp_ref[:, ks], kc_ref[:, ks], kn_ref[:, ks]], axis=0)
        vh = jnp.concatenate([vp_ref[:, ks], vc_ref[:, ks], vn_ref[:, ks]], axis=0)
        s = _dot_nt(qh, kh) * scale2 + bias
        sink = jnp.full((rows, 1), sink_ref[h * A_GROUP] * log2e, F32)
        for g in range(1, A_GROUP):
            sink = jnp.where(grp == g, sink_ref[h * A_GROUP + g] * log2e, sink)
        m = jnp.maximum(jnp.max(s, axis=-1, keepdims=True), sink)
        p = jnp.exp2(s - m)
        denom = jnp.sum(p, axis=-1, keepdims=True) + jnp.exp2(sink - m)
        o = jnp.dot(p.astype(BF16), vh, preferred_element_type=F32) * (1.0 / denom)
        for g in range(A_GROUP):
            c0 = (h * A_GROUP + g) * A_HEAD_DIM
            o_ref[:, c0:c0 + A_HEAD_DIM] = o[g * BLK:(g + 1) * BLK].astype(BF16)


def _win_attn(q, k, v, sink, batch, seq):
    t = q.shape[0]
    nb = seq // BLK
    q_cols = A_HEADS * A_HEAD_DIM
    k_cols = A_KV_HEADS * A_HEAD_DIM
    cur = lambda b, n: (b * nb + n, 0)
    prev = lambda b, n: (b * nb + jnp.maximum(n - 1, 0), 0)
    nxt = lambda b, n: (b * nb + jnp.minimum(n + 1, nb - 1), 0)
    kv_spec = lambda im: pl.BlockSpec((BLK, k_cols), im)
    return pl.pallas_call(
        functools.partial(_win_attn_kernel, nb=nb),
        grid=(batch, nb),
        in_specs=[
            pl.BlockSpec(memory_space=pltpu.SMEM),
            pl.BlockSpec((BLK, q_cols), cur),
            kv_spec(prev), kv_spec(cur), kv_spec(nxt),
            kv_spec(prev), kv_spec(cur), kv_spec(nxt),
        ],
        out_specs=pl.BlockSpec((BLK, q_cols), cur),
        out_shape=jax.ShapeDtypeStruct((t, q_cols), BF16),
        compiler_params=_compiler_params(("parallel", "parallel")),
        name="win_attn",
    )(sink, q, k, k, k, v, v, v)


def _proj_res_ln_kernel(a_ref, w_ref, x_ref, g_ref, b_ref, o_ref):
    h = jnp.dot(a_ref[...], w_ref[...], preferred_element_type=F32)
    o_ref[...] = _layer_norm_rows(ALPHA * x_ref[...] + h, g_ref[...], b_ref[...])


def _proj_res_ln(a, w_stack, layer, x, g, b, tm):
    t, kdim = a.shape
    row = lambda i: (i, 0)
    const = lambda i: (0, 0)
    return pl.pallas_call(
        _proj_res_ln_kernel,
        grid=(t // tm,),
        in_specs=[
            pl.BlockSpec((tm, kdim), row),
            pl.BlockSpec((None, kdim, D_MODEL), lambda i: (layer, 0, 0)),
            pl.BlockSpec((tm, D_MODEL), row),
            pl.BlockSpec((1, D_MODEL), const),
            pl.BlockSpec((1, D_MODEL), const),
        ],
        out_specs=pl.BlockSpec((tm, D_MODEL), row),
        out_shape=jax.ShapeDtypeStruct((t, D_MODEL), F32),
        compiler_params=_compiler_params(("parallel",)),
        name="proj_res_ln",
    )(a, w_stack, x, g, b)


def _mlp_kernel(x_ref, wu_ref, wd_ref, g_ref, b_ref, o_ref, xb_ref):
    j = pl.program_id(1)

    @pl.when(j == 0)
    def _():
        xb_ref[...] = x_ref[...].astype(BF16)
        o_ref[...] = jnp.zeros_like(o_ref)

    h = jnp.dot(xb_ref[...], wu_ref[...], preferred_element_type=F32)
    h = jnp.square(jnp.maximum(h, 0.0)).astype(BF16)
    o_ref[...] += jnp.dot(h, wd_ref[...], preferred_element_type=F32)

    @pl.when(j == pl.num_programs(1) - 1)
    def _():
        o_ref[...] = _layer_norm_rows(ALPHA * x_ref[...] + o_ref[...], g_ref[...], b_ref[...])


def _mlp(x, wu_stack, wd_stack, layer, g, b, tm, tf):
    t = x.shape[0]
    row = lambda i, j: (i, 0)
    const = lambda i, j: (0, 0)
    return pl.pallas_call(
        _mlp_kernel,
        grid=(t // tm, D_FF // tf),
        in_specs=[
            pl.BlockSpec((tm, D_MODEL), row),
            pl.BlockSpec((None, D_MODEL, tf), lambda i, j: (layer, 0, j)),
            pl.BlockSpec((None, tf, D_MODEL), lambda i, j: (layer, j, 0)),
            pl.BlockSpec((1, D_MODEL), const),
            pl.BlockSpec((1, D_MODEL), const),
        ],
        out_specs=pl.BlockSpec((tm, D_MODEL), row),
        out_shape=jax.ShapeDtypeStruct((t, D_MODEL), F32),
        scratch_shapes=[pltpu.VMEM((tm, D_MODEL), BF16)],
        compiler_params=_compiler_params(("parallel", "arbitrary")),
        name="mlp",
    )(x, wu_stack, wd_stack, g, b)


def _mla_proj_kernel(x_ref, wd_ref, gq_ref, gkv_ref, wuq_ref, wuk_ref, wuvt_ref, tab_ref,
                     q_ref, k_ref, vt_ref):
    xb = x_ref[...].astype(BF16)
    tab = tab_ref[...]
    lane = lax.broadcasted_iota(jnp.int32, tab.shape, 1)

    def rope_pair(y):
        t = y * tab
        return jnp.where(lane < ROPE_DIM, t + pltpu.roll(t, ROPE_DIM, 1), 0.0).astype(BF16)

    lat = jnp.dot(xb, wd_ref[...], preferred_element_type=F32)
    cq = lat[:, :Q_LORA_PAD]
    cq = cq * lax.rsqrt(jnp.sum(cq * cq, axis=-1, keepdims=True) * (1.0 / Q_LORA) + RMS_EPS)
    cq = (cq * gq_ref[...]).astype(BF16)
    ckv = lat[:, Q_LORA_PAD:Q_LORA_PAD + KV_LORA]
    ckv = ckv * lax.rsqrt(jnp.mean(ckv * ckv, axis=-1, keepdims=True) + RMS_EPS)
    ckv = (ckv * gkv_ref[...]).astype(BF16)
    kr = rope_pair(lat[:, Q_LORA_PAD + KV_LORA:])

    heads_per_chunk = 2
    qchunk = heads_per_chunk * QK_SLAB
    for c in range(0, B_HEADS * QK_SLAB, qchunk):
        qq = jnp.dot(cq, wuq_ref[:, c:c + qchunk], preferred_element_type=F32)
        for hh in range(heads_per_chunk):
            o = hh * QK_SLAB
            q_ref[:, c + o:c + o + NOPE_DIM] = qq[:, o:o + NOPE_DIM].astype(BF16)
            q_ref[:, c + o + NOPE_DIM:c + o + QK_SLAB] = rope_pair(qq[:, o + NOPE_DIM:o + QK_SLAB])

    kchunk = 4 * NOPE_DIM
    for c in range(0, B_HEADS * NOPE_DIM, kchunk):
        kn = jnp.dot(ckv, wuk_ref[:, c:c + kchunk], preferred_element_type=F32)
        for hh in range(kchunk // NOPE_DIM):
            h = c // NOPE_DIM + hh
            k_ref[:, h * QK_SLAB:h * QK_SLAB + NOPE_DIM] = (
                kn[:, hh * NOPE_DIM:(hh + 1) * NOPE_DIM].astype(BF16))
            k_ref[:, h * QK_SLAB + NOPE_DIM:(h + 1) * QK_SLAB] = kr
    ones = jnp.ones((VT_SLAB - V_DIM, ckv.shape[0]), BF16)
    for c in range(0, B_HEADS * V_DIM, kchunk):
        vt = _dot_nt(wuvt_ref[c:c + kchunk, :], ckv).astype(BF16)
        for hh in range(kchunk // V_DIM):
            r0 = (c // V_DIM + hh) * VT_SLAB
            vt_ref[r0:r0 + V_DIM, :] = vt[hh * V_DIM:(hh + 1) * V_DIM]
            vt_ref[r0 + V_DIM:r0 + VT_SLAB, :] = ones


def _mla_proj(x, wd, gq, gkv, wuq, wuk, wuvt, tab, seq, tm):
    t = x.shape[0]
    tiles_per_seq = seq // tm
    row = lambda i: (i, 0)
    const = lambda i: (0, 0)
    full = lambda a: pl.BlockSpec(a.shape, const)
    return pl.pallas_call(
        _mla_proj_kernel,
        grid=(t // tm,),
        in_specs=[
            pl.BlockSpec((tm, D_MODEL), row),
            full(wd), full(gq), full(gkv), full(wuq), full(wuk), full(wuvt),
            pl.BlockSpec((tm, LANES), lambda i: (i % tiles_per_seq, 0)),
        ],
        out_specs=[
            pl.BlockSpec((tm, B_HEADS * QK_SLAB), row),
            pl.BlockSpec((tm, B_HEADS * QK_SLAB), row),
            pl.BlockSpec((B_HEADS * VT_SLAB, tm), lambda i: (0, i)),
        ],
        out_shape=[
            jax.ShapeDtypeStruct((t, B_HEADS * QK_SLAB), BF16),
            jax.ShapeDtypeStruct((t, B_HEADS * QK_SLAB), BF16),
            jax.ShapeDtypeStruct((B_HEADS * VT_SLAB, t), BF16),
        ],
        compiler_params=_compiler_params(("parallel",)),
        name="mla_proj",
    )(x, wd, gq, gkv, wuq, wuk, wuvt, tab)


F32_SUBLANES = 8
MLA_EXP2_SCALE = math.log2(math.e) / math.sqrt(NOPE_DIM + ROPE_DIM)


def _mla_attn_kernel(q_ref, k_ref, vt_ref, o_ref, s_ref, acc_ref, *, tk1, tk2, unroll):
    seq = k_ref.shape[0]
    tq = q_ref.shape[0]
    sub = F32_SUBLANES
    q = q_ref[...]

    m8 = jnp.full((sub, tq), -jnp.inf, F32)
    for c0 in range(0, seq, tk1):
        s = _dot_nt(k_ref[c0:c0 + tk1, :], q) * MLA_EXP2_SCALE
        s_ref[c0:c0 + tk1, :] = s
        m8 = jnp.maximum(m8, jnp.max(s.reshape(tk1 // sub, sub, tq), axis=0))
    m = jnp.max(m8, axis=0, keepdims=True)

    acc_ref[...] = jnp.zeros_like(acc_ref)

    def body(c, carry):
        start = pl.multiple_of(c * tk2, tk2)
        p = jnp.exp2((s_ref[pl.ds(start, tk2), :] - m).astype(BF16))
        acc_ref[...] += jnp.dot(vt_ref[:, pl.ds(start, tk2)], p, preferred_element_type=F32)
        return carry

    lax.fori_loop(0, seq // tk2, body, 0, unroll=unroll)
    inv_l = 1.0 / acc_ref[V_DIM:V_DIM + 1, :]
    o_ref[...] = (acc_ref[:V_DIM, :] * inv_l).T.astype(BF16)


def _mla_attn(q, k, vt, batch, seq, tq, tk1, tk2, unroll):
    t = q.shape[0]
    nq = seq // tq
    return pl.pallas_call(
        functools.partial(_mla_attn_kernel, tk1=tk1, tk2=tk2, unroll=unroll),
        grid=(batch, B_HEADS, nq),
        in_specs=[
            pl.BlockSpec((tq, QK_SLAB), lambda b, h, i: (b * nq + i, h)),
            pl.BlockSpec((seq, QK_SLAB), lambda b, h, i: (b, h)),
            pl.BlockSpec((VT_SLAB, seq), lambda b, h, i: (h, b)),
        ],
        out_specs=pl.BlockSpec((tq, V_DIM), lambda b, h, i: (b * nq + i, h)),
        out_shape=jax.ShapeDtypeStruct((t, B_HEADS * V_DIM), BF16),
        scratch_shapes=[
            pltpu.VMEM((seq, tq), F32),
            pltpu.VMEM((VT_SLAB, tq), F32),
        ],
        compiler_params=_compiler_params(("parallel", "parallel", "arbitrary")),
        name="mla_attn",
    )(q, k, vt)


def _rope_angles(seq, dim):
    inv = 1.0 / (ROPE_THETA ** (jnp.arange(0, dim, 2, dtype=F32) / dim))
    ang = jnp.arange(seq, dtype=F32)[:, None] * inv[None, :]
    return jnp.concatenate([ang, ang], axis=-1)


def _rot_half_cols(w):
    half = w.shape[-1] // 2
    return jnp.concatenate([-w[:, half:], w[:, :half]], axis=-1)


def _prep_mla_weights(w_dqkv, q_norm, kv_norm, w_uq, w_ukv):
    wq = w_dqkv[:, :Q_LORA]
    wkv = w_dqkv[:, Q_LORA:Q_LORA + KV_LORA]
    wr = w_dqkv[:, Q_LORA + KV_LORA:]
    zpad = jnp.zeros((D_MODEL, Q_LORA_PAD - Q_LORA), w_dqkv.dtype)
    wd = jnp.concatenate([wq, zpad, wkv, wr, _rot_half_cols(wr)], axis=1).astype(BF16)
    gq = jnp.pad(q_norm, (0, Q_LORA_PAD - Q_LORA)).reshape(1, Q_LORA_PAD)
    gkv = kv_norm.reshape(1, KV_LORA)
    uq = w_uq.reshape(Q_LORA, B_HEADS, NOPE_DIM + ROPE_DIM)
    uq_r = uq[:, :, NOPE_DIM:]
    uq_rot = jnp.concatenate([-uq_r[..., ROPE_DIM // 2:], uq_r[..., :ROPE_DIM // 2]], axis=-1)
    wuq = jnp.concatenate([uq, uq_rot], axis=-1).reshape(Q_LORA, B_HEADS * QK_SLAB)
    wuq = jnp.pad(wuq, ((0, Q_LORA_PAD - Q_LORA), (0, 0))).astype(BF16)
    ukv = w_ukv.reshape(KV_LORA, B_HEADS, NOPE_DIM + V_DIM)
    wuk = ukv[:, :, :NOPE_DIM].reshape(KV_LORA, B_HEADS * NOPE_DIM).astype(BF16)
    wuvt = ukv[:, :, NOPE_DIM:].reshape(KV_LORA, B_HEADS * V_DIM).T.astype(BF16)
    return wd, gq, gkv, wuq, wuk, wuvt


def _tiles(seq):
    tm = 512
    assert seq % tm == 0 and seq % BLK == 0
    tk2 = 512
    unroll = max(1, min(8, seq // tk2 // 2))
    return dict(tm=tm, tf=1024, tq=1024, tk1=min(seq, 2048), tk2=tk2, unroll=unroll)


def _trunk(x3, weights):
    batch, seq, _ = x3.shape
    cfg = _tiles(seq)
    tm = cfg["tm"]
    x = x3.reshape(batch * seq, D_MODEL)

    ang_a = _rope_angles(seq, A_HEAD_DIM)
    sign = jnp.where(jnp.arange(A_HEAD_DIM) < A_HEAD_DIM // 2, -1.0, 1.0).astype(F32)
    cos_a, sin_a = jnp.cos(ang_a), jnp.sin(ang_a) * sign
    ang_b = _rope_angles(seq, ROPE_DIM)
    tab_b = jnp.concatenate([jnp.cos(ang_b), jnp.sin(ang_b)], axis=-1)

    for i in range(DEPTH):
        j = i // N_MIXERS
        if i % N_MIXERS == 0:
            q, k, v = _qkv_rope(x, weights["w_qkv_a"], j, cos_a, sin_a, seq, tm)
            attn = _win_attn(q, k, v, weights["sink_a"][j], batch, seq)
            w_o = weights["w_o_a"]
        else:
            q, k, v = _mla_proj(x, *weights["mla"][j], tab_b, seq, tm)
            attn = _mla_attn(q, k, v, batch, seq, cfg["tq"], cfg["tk1"], cfg["tk2"],
                             cfg["unroll"])
            w_o = weights["w_o_b"]
        g, b = weights["ln_g"], weights["ln_b"]
        x = _proj_res_ln(attn, w_o, j, x, g[i, 0:1], b[i, 0:1], tm)
        x = _mlp(x, weights["w_up"], weights["w_down"], i, g[i, 1:2], b[i, 1:2],
                 tm, cfg["tf"])
    return x.reshape(batch, seq, D_MODEL)


def kernel(x_prompt, x_sample, w_qkv_a, sink_a, w_o_a, w_dqkv_b, q_norm_b, kv_norm_b,
           w_uq_b, w_ukv_b, w_o_b, w_up, w_down, ln_g, ln_b):
    weights = dict(
        w_qkv_a=w_qkv_a.astype(BF16),
        sink_a=sink_a.astype(F32),
        w_o_a=w_o_a.astype(BF16),
        mla=[_prep_mla_weights(w_dqkv_b[j], q_norm_b[j], kv_norm_b[j], w_uq_b[j], w_ukv_b[j])
             for j in range(w_dqkv_b.shape[0])],
        w_o_b=w_o_b.astype(BF16),
        w_up=w_up.astype(BF16),
        w_down=w_down.astype(BF16),
        ln_g=ln_g.astype(F32),
        ln_b=ln_b.astype(F32),
    )
    return (_trunk(x_prompt, weights), _trunk(x_sample, weights))
```

```python
import functools
import math

import jax
import jax.numpy as jnp
from jax import lax
from jax.experimental import pallas as pl
from jax.experimental.pallas import tpu as pltpu

D_MODEL = 2048
DEPTH = 4
N_MIXERS = 2
BLK = 128
A_HEADS = 16
A_KV_HEADS = 4
A_GROUP = A_HEADS // A_KV_HEADS
A_HEAD_DIM = D_MODEL // A_HEADS
WINDOW = 128
B_HEADS = 16
Q_LORA = 448
KV_LORA = 128
NOPE_DIM = 128
ROPE_DIM = 64
V_DIM = 128
D_FF = 4 * D_MODEL
ROPE_THETA = 10000.0
LN_EPS = 1e-5
RMS_EPS = 1e-6
ALPHA = (2 * DEPTH) ** 0.25

LANES = 128
MXU_DIM = 256
VMEM_LIMIT_BYTES = 56 * 1024 * 1024

Q_LORA_PAD = 512
LAT_COLS = Q_LORA_PAD + KV_LORA + 2 * ROPE_DIM
QK_SLAB = MXU_DIM
BF16_SUBLANES = 16
VT_SLAB = V_DIM + BF16_SUBLANES

F32 = jnp.float32
BF16 = jnp.bfloat16


def _compiler_params(semantics):
    return pltpu.CompilerParams(dimension_semantics=semantics,
                                vmem_limit_bytes=VMEM_LIMIT_BYTES)


def _layer_norm_rows(y, g, b):
    mu = jnp.mean(y, axis=-1, keepdims=True)
    d = y - mu
    var = jnp.mean(d * d, axis=-1, keepdims=True)
    return d * lax.rsqrt(var + LN_EPS) * g + b


def _dot_nt(a, b):
    return lax.dot_general(a, b, (((1,), (1,)), ((), ())), preferred_element_type=F32)


def _qkv_rope_kernel(x_ref, w_ref, wvt_ref, cos_ref, sin_ref, q_ref, k_ref, vt_ref):
    xb = x_ref[...].astype(BF16)
    cos = cos_ref[...]
    sin = sin_ref[...]
    chunk = A_GROUP * A_HEAD_DIM
    q_cols = A_HEADS * A_HEAD_DIM
    k_cols = A_KV_HEADS * A_HEAD_DIM

    def rope(t):
        return t * cos + pltpu.roll(t, A_HEAD_DIM // 2, 1) * sin

    for c in range(0, q_cols + k_cols, chunk):
        acc = jnp.dot(xb, w_ref[:, c:c + chunk], preferred_element_type=F32)
        for hh in range(chunk // A_HEAD_DIM):
            y = rope(acc[:, hh * A_HEAD_DIM:(hh + 1) * A_HEAD_DIM]).astype(BF16)
            col = c + hh * A_HEAD_DIM
            if col < q_cols:
                q_ref[:, col:col + A_HEAD_DIM] = y
            else:
                k_ref[:, col - q_cols:col - q_cols + A_HEAD_DIM] = y
    vt = _dot_nt(wvt_ref[...], xb).astype(BF16)
    ones = jnp.ones((VT_SLAB - V_DIM, xb.shape[0]), BF16)
    for h in range(A_KV_HEADS):
        vt_ref[h * VT_SLAB:h * VT_SLAB + A_HEAD_DIM, :] = vt[h * A_HEAD_DIM:(h + 1) * A_HEAD_DIM]
        vt_ref[h * VT_SLAB + A_HEAD_DIM:(h + 1) * VT_SLAB, :] = ones


def _qkv_rope(x, w_stack, wvt_stack, layer, cos, sin, seq, tm):
    t = x.shape[0]
    tiles_per_seq = seq // tm
    q_cols = A_HEADS * A_HEAD_DIM
    k_cols = A_KV_HEADS * A_HEAD_DIM
    row = lambda i: (i, 0)
    pos = lambda i: (i % tiles_per_seq, 0)
    return pl.pallas_call(
        _qkv_rope_kernel,
        grid=(t // tm,),
        in_specs=[
            pl.BlockSpec((tm, D_MODEL), row),
            pl.BlockSpec((None, D_MODEL, q_cols + 2 * k_cols), lambda i: (layer, 0, 0)),
            pl.BlockSpec((None, k_cols, D_MODEL), lambda i: (layer, 0, 0)),
            pl.BlockSpec((tm, A_HEAD_DIM), pos),
            pl.BlockSpec((tm, A_HEAD_DIM), pos),
        ],
        out_specs=[
            pl.BlockSpec((tm, q_cols), row),
            pl.BlockSpec((tm, k_cols), row),
            pl.BlockSpec((A_KV_HEADS * VT_SLAB, tm), lambda i: (0, i)),
        ],
        out_shape=[
            jax.ShapeDtypeStruct((t, q_cols), BF16),
            jax.ShapeDtypeStruct((t, k_cols), BF16),
            jax.ShapeDtypeStruct((A_KV_HEADS * VT_SLAB, t), BF16),
        ],
        compiler_params=_compiler_params(("parallel",)),
        name="qkv_rope",
    )(x, w_stack, wvt_stack, cos, sin)


def _win_attn_kernel(sink_ref, q_ref, kp_ref, kc_ref, kn_ref, vp_ref, vc_ref, vn_ref,
                     o_ref, *, nb):
    n = pl.program_id(1)
    cols = A_GROUP * BLK
    span = BLK + 2 * WINDOW
    key = lax.broadcasted_iota(jnp.int32, (span, cols), 0)
    qi = lax.broadcasted_iota(jnp.int32, (span, cols), 1) & (BLK - 1)
    lo = jnp.where(n > 0, 0, WINDOW)
    hi = jnp.where(n < nb - 1, span, BLK + WINDOW)
    valid = (key >= jnp.maximum(qi, lo)) & (key <= qi + 2 * WINDOW) & (key < hi)
    bias = jnp.where(valid, 0.0, -jnp.inf).astype(F32)
    grp = lax.broadcasted_iota(jnp.int32, (1, cols), 1) // BLK
    log2e = math.log2(math.e)
    scale2 = log2e / math.sqrt(A_HEAD_DIM)

    scores = []
    for h in range(A_KV_HEADS):
        ks = slice(h * A_HEAD_DIM, (h + 1) * A_HEAD_DIM)
        qh = jnp.concatenate(
            [q_ref[:, (h * A_GROUP + g) * A_HEAD_DIM:(h * A_GROUP + g + 1) * A_HEAD_DIM]
             for g in range(A_GROUP)], axis=0)
        kh = jnp.concatenate([kp_ref[:, ks], kc_ref[:, ks], kn_ref[:, ks]], axis=0)
        scores.append(_dot_nt(kh, qh) * scale2 + bias)

    for h in range(A_KV_HEADS):
        vs = slice(h * VT_SLAB, (h + 1) * VT_SLAB)
        vth = jnp.concatenate([vp_ref[vs, :], vc_ref[vs, :], vn_ref[vs, :]], axis=1)
        s = scores[h]
        sink = jnp.full((1, cols), sink_ref[h * A_GROUP] * log2e, F32)
        for g in range(1, A_GROUP):
            sink = jnp.where(grp == g, sink_ref[h * A_GROUP + g] * log2e, sink)
        m = jnp.maximum(jnp.max(s, axis=0, keepdims=True), sink)
        p = jnp.exp2((s - m).astype(BF16))
        acc = jnp.dot(vth, p, preferred_element_type=F32)
        denom = acc[A_HEAD_DIM:A_HEAD_DIM + 1, :] + jnp.exp2(sink - m)
        o = (acc[:A_HEAD_DIM, :] * (1.0 / denom)).T
        for g in range(A_GROUP):
            c0 = (h * A_GROUP + g) * A_HEAD_DIM
            o_ref[:, c0:c0 + A_HEAD_DIM] = o[g * BLK:(g + 1) * BLK].astype(BF16)


def _win_attn(q, k, vt, sink, batch, seq):
    t = q.shape[0]
    nb = seq // BLK
    q_cols = A_HEADS * A_HEAD_DIM
    k_cols = A_KV_HEADS * A_HEAD_DIM
    cur = lambda b, n: b * nb + n
    prev = lambda b, n: b * nb + jnp.maximum(n - 1, 0)
    nxt = lambda b, n: b * nb + jnp.minimum(n + 1, nb - 1)
    k_spec = lambda blk: pl.BlockSpec((BLK, k_cols), lambda b, n: (blk(b, n), 0))
    vt_spec = lambda blk: pl.BlockSpec((A_KV_HEADS * VT_SLAB, BLK), lambda b, n: (0, blk(b, n)))
    return pl.pallas_call(
        functools.partial(_win_attn_kernel, nb=nb),
        grid=(batch, nb),
        in_specs=[
            pl.BlockSpec(memory_space=pltpu.SMEM),
            pl.BlockSpec((BLK, q_cols), lambda b, n: (cur(b, n), 0)),
            k_spec(prev), k_spec(cur), k_spec(nxt),
            vt_spec(prev), vt_spec(cur), vt_spec(nxt),
        ],
        out_specs=pl.BlockSpec((BLK, q_cols), lambda b, n: (cur(b, n), 0)),
        out_shape=jax.ShapeDtypeStruct((t, q_cols), BF16),
        compiler_params=_compiler_params(("parallel", "parallel")),
        name="win_attn",
    )(sink, q, k, k, k, vt, vt, vt)


def _proj_res_ln_kernel(a_ref, w_ref, x_ref, g_ref, b_ref, o_ref):
    h = jnp.dot(a_ref[...], w_ref[...], preferred_element_type=F32)
    o_ref[...] = _layer_norm_rows(ALPHA * x_ref[...] + h, g_ref[...], b_ref[...])


def _proj_res_ln(a, w_stack, layer, x, g, b, tm):
    t, kdim = a.shape
    row = lambda i: (i, 0)
    const = lambda i: (0, 0)
    return pl.pallas_call(
        _proj_res_ln_kernel,
        grid=(t // tm,),
        in_specs=[
            pl.BlockSpec((tm, kdim), row),
            pl.BlockSpec((None, kdim, D_MODEL), lambda i: (layer, 0, 0)),
            pl.BlockSpec((tm, D_MODEL), row),
            pl.BlockSpec((1, D_MODEL), const),
            pl.BlockSpec((1, D_MODEL), const),
        ],
        out_specs=pl.BlockSpec((tm, D_MODEL), row),
        out_shape=jax.ShapeDtypeStruct((t, D_MODEL), F32),
        compiler_params=_compiler_params(("parallel",)),
        name="proj_res_ln",
    )(a, w_stack, x, g, b)


def _mlp_kernel(x_ref, wu_ref, wd_ref, g_ref, b_ref, o_ref, xb_ref):
    j = pl.program_id(1)

    @pl.when(j == 0)
    def _():
        xb_ref[...] = x_ref[...].astype(BF16)
        o_ref[...] = jnp.zeros_like(o_ref)

    h = jnp.dot(xb_ref[...], wu_ref[...], preferred_element_type=F32)
    h = jnp.square(jnp.maximum(h, 0.0)).astype(BF16)
    o_ref[...] += jnp.dot(h, wd_ref[...], preferred_element_type=F32)

    @pl.when(j == pl.num_programs(1) - 1)
    def _():
        o_ref[...] = _layer_norm_rows(ALPHA * x_ref[...] + o_ref[...], g_ref[...], b_ref[...])


def _mlp(x, wu_stack, wd_stack, layer, g, b, tm, tf):
    t = x.shape[0]
    row = lambda i, j: (i, 0)
    const = lambda i, j: (0, 0)
    return pl.pallas_call(
        _mlp_kernel,
        grid=(t // tm, D_FF // tf),
        in_specs=[
            pl.BlockSpec((tm, D_MODEL), row),
            pl.BlockSpec((None, D_MODEL, tf), lambda i, j: (layer, 0, j)),
            pl.BlockSpec((None, tf, D_MODEL), lambda i, j: (layer, j, 0)),
            pl.BlockSpec((1, D_MODEL), const),
            pl.BlockSpec((1, D_MODEL), const),
        ],
        out_specs=pl.BlockSpec((tm, D_MODEL), row),
        out_shape=jax.ShapeDtypeStruct((t, D_MODEL), F32),
        scratch_shapes=[pltpu.VMEM((tm, D_MODEL), BF16)],
        compiler_params=_compiler_params(("parallel", "arbitrary")),
        name="mlp",
    )(x, wu_stack, wd_stack, g, b)


def _mla_proj_kernel(x_ref, wd_ref, gq_ref, gkv_ref, wuq_ref, wuk_ref, wuvt_ref, tab_ref,
                     q_ref, k_ref, vt_ref):
    xb = x_ref[...].astype(BF16)
    tab = tab_ref[...]
    lane = lax.broadcasted_iota(jnp.int32, tab.shape, 1)

    def rope_pair(y):
        t = y * tab
        return jnp.where(lane < ROPE_DIM, t + pltpu.roll(t, ROPE_DIM, 1), 0.0).astype(BF16)

    lat = jnp.dot(xb, wd_ref[...], preferred_element_type=F32)
    cq = lat[:, :Q_LORA_PAD]
    cq = cq * lax.rsqrt(jnp.sum(cq * cq, axis=-1, keepdims=True) * (1.0 / Q_LORA) + RMS_EPS)
    cq = (cq * gq_ref[...]).astype(BF16)
    ckv = lat[:, Q_LORA_PAD:Q_LORA_PAD + KV_LORA]
    ckv = ckv * lax.rsqrt(jnp.mean(ckv * ckv, axis=-1, keepdims=True) + RMS_EPS)
    ckv = (ckv * gkv_ref[...]).astype(BF16)
    kr = rope_pair(lat[:, Q_LORA_PAD + KV_LORA:])

    heads_per_chunk = 2
    qchunk = heads_per_chunk * QK_SLAB
    for c in range(0, B_HEADS * QK_SLAB, qchunk):
        qq = jnp.dot(cq, wuq_ref[:, c:c + qchunk], preferred_element_type=F32)
        for hh in range(heads_per_chunk):
            o = hh * QK_SLAB
            q_ref[:, c + o:c + o + NOPE_DIM] = qq[:, o:o + NOPE_DIM].astype(BF16)
            q_ref[:, c + o + NOPE_DIM:c + o + QK_SLAB] = rope_pair(qq[:, o + NOPE_DIM:o + QK_SLAB])

    kchunk = 4 * NOPE_DIM
    for c in range(0, B_HEADS * NOPE_DIM, kchunk):
        kn = jnp.dot(ckv, wuk_ref[:, c:c + kchunk], preferred_element_type=F32)
        for hh in range(kchunk // NOPE_DIM):
            h = c // NOPE_DIM + hh
            k_ref[:, h * QK_SLAB:h * QK_SLAB + NOPE_DIM] = (
                kn[:, hh * NOPE_DIM:(hh + 1) * NOPE_DIM].astype(BF16))
            k_ref[:, h * QK_SLAB + NOPE_DIM:(h + 1) * QK_SLAB] = kr
    ones = jnp.ones((VT_SLAB - V_DIM, ckv.shape[0]), BF16)
    for c in range(0, B_HEADS * V_DIM, kchunk):
        vt = _dot_nt(wuvt_ref[c:c + kchunk, :], ckv).astype(BF16)
        for hh in range(kchunk // V_DIM):
            r0 = (c // V_DIM + hh) * VT_SLAB
            vt_ref[r0:r0 + V_DIM, :] = vt[hh * V_DIM:(hh + 1) * V_DIM]
            vt_ref[r0 + V_DIM:r0 + VT_SLAB, :] = ones


def _mla_proj(x, wd, gq, gkv, wuq, wuk, wuvt, tab, seq, tm):
    t = x.shape[0]
    tiles_per_seq = seq // tm
    row = lambda i: (i, 0)
    const = lambda i: (0, 0)
    full = lambda a: pl.BlockSpec(a.shape, const)
    return pl.pallas_call(
        _mla_proj_kernel,
        grid=(t // tm,),
        in_specs=[
            pl.BlockSpec((tm, D_MODEL), row),
            full(wd), full(gq), full(gkv), full(wuq), full(wuk), full(wuvt),
            pl.BlockSpec((tm, LANES), lambda i: (i % tiles_per_seq, 0)),
        ],
        out_specs=[
            pl.BlockSpec((tm, B_HEADS * QK_SLAB), row),
            pl.BlockSpec((tm, B_HEADS * QK_SLAB), row),
            pl.BlockSpec((B_HEADS * VT_SLAB, tm), lambda i: (0, i)),
        ],
        out_shape=[
            jax.ShapeDtypeStruct((t, B_HEADS * QK_SLAB), BF16),
            jax.ShapeDtypeStruct((t, B_HEADS * QK_SLAB), BF16),
            jax.ShapeDtypeStruct((B_HEADS * VT_SLAB, t), BF16),
        ],
        compiler_params=_compiler_params(("parallel",)),
        name="mla_proj",
    )(x, wd, gq, gkv, wuq, wuk, wuvt, tab)


F32_SUBLANES = 8
MLA_EXP2_SCALE = math.log2(math.e) / math.sqrt(NOPE_DIM + ROPE_DIM)


def _mla_attn_kernel(q_ref, k_ref, vt_ref, o_ref, s_ref, acc_ref, *, tk1, tk2, unroll):
    seq = k_ref.shape[0]
    tq = q_ref.shape[0]
    sub = F32_SUBLANES
    q = q_ref[...]

    m8 = jnp.full((sub, tq), -jnp.inf, F32)
    for c0 in range(0, seq, tk1):
        s = _dot_nt(k_ref[c0:c0 + tk1, :], q) * MLA_EXP2_SCALE
        s_ref[c0:c0 + tk1, :] = s
        m8 = jnp.maximum(m8, jnp.max(s.reshape(tk1 // sub, sub, tq), axis=0))
    m = jnp.max(m8, axis=0, keepdims=True)

    acc_ref[...] = jnp.zeros_like(acc_ref)

    def body(c, carry):
        start = pl.multiple_of(c * tk2, tk2)
        p = jnp.exp2((s_ref[pl.ds(start, tk2), :] - m).astype(BF16))
        acc_ref[...] += jnp.dot(vt_ref[:, pl.ds(start, tk2)], p, preferred_element_type=F32)
        return carry

    lax.fori_loop(0, seq // tk2, body, 0, unroll=unroll)
    inv_l = 1.0 / acc_ref[V_DIM:V_DIM + 1, :]
    o_ref[...] = (acc_ref[:V_DIM, :] * inv_l).T.astype(BF16)


def _mla_attn(q, k, vt, batch, seq, tq, tk1, tk2, unroll):
    t = q.shape[0]
    nq = seq // tq
    return pl.pallas_call(
        functools.partial(_mla_attn_kernel, tk1=tk1, tk2=tk2, unroll=unroll),
        grid=(batch, B_HEADS, nq),
        in_specs=[
            pl.BlockSpec((tq, QK_SLAB), lambda b, h, i: (b * nq + i, h)),
            pl.BlockSpec((seq, QK_SLAB), lambda b, h, i: (b, h)),
            pl.BlockSpec((VT_SLAB, seq), lambda b, h, i: (h, b)),
        ],
        out_specs=pl.BlockSpec((tq, V_DIM), lambda b, h, i: (b * nq + i, h)),
        out_shape=jax.ShapeDtypeStruct((t, B_HEADS * V_DIM), BF16),
        scratch_shapes=[
            pltpu.VMEM((seq, tq), F32),
            pltpu.VMEM((VT_SLAB, tq), F32),
        ],
        compiler_params=_compiler_params(("parallel", "parallel", "arbitrary")),
        name="mla_attn",
    )(q, k, vt)


def _rope_angles(seq, dim):
    inv = 1.0 / (ROPE_THETA ** (jnp.arange(0, dim, 2, dtype=F32) / dim))
    ang = jnp.arange(seq, dtype=F32)[:, None] * inv[None, :]
    return jnp.concatenate([ang, ang], axis=-1)


def _rot_half_cols(w):
    half = w.shape[-1] // 2
    return jnp.concatenate([-w[:, half:], w[:, :half]], axis=-1)


def _prep_mla_weights(w_dqkv, q_norm, kv_norm, w_uq, w_ukv):
    wq = w_dqkv[:, :Q_LORA]
    wkv = w_dqkv[:, Q_LORA:Q_LORA + KV_LORA]
    wr = w_dqkv[:, Q_LORA + KV_LORA:]
    zpad = jnp.zeros((D_MODEL, Q_LORA_PAD - Q_LORA), w_dqkv.dtype)
    wd = jnp.concatenate([wq, zpad, wkv, wr, _rot_half_cols(wr)], axis=1).astype(BF16)
    gq = jnp.pad(q_norm, (0, Q_LORA_PAD - Q_LORA)).reshape(1, Q_LORA_PAD)
    gkv = kv_norm.reshape(1, KV_LORA)
    uq = w_uq.reshape(Q_LORA, B_HEADS, NOPE_DIM + ROPE_DIM)
    uq_r = uq[:, :, NOPE_DIM:]
    uq_rot = jnp.concatenate([-uq_r[..., ROPE_DIM // 2:], uq_r[..., :ROPE_DIM // 2]], axis=-1)
    wuq = jnp.concatenate([uq, uq_rot], axis=-1).reshape(Q_LORA, B_HEADS * QK_SLAB)
    wuq = jnp.pad(wuq, ((0, Q_LORA_PAD - Q_LORA), (0, 0))).astype(BF16)
    ukv = w_ukv.reshape(KV_LORA, B_HEADS, NOPE_DIM + V_DIM)
    wuk = ukv[:, :, :NOPE_DIM].reshape(KV_LORA, B_HEADS * NOPE_DIM).astype(BF16)
    wuvt = ukv[:, :, NOPE_DIM:].reshape(KV_LORA, B_HEADS * V_DIM).T.astype(BF16)
    return wd, gq, gkv, wuq, wuk, wuvt


def _tiles(seq):
    tm = 512
    assert seq % tm == 0 and seq % BLK == 0
    tk2 = 512
    unroll = max(1, min(8, seq // tk2 // 2))
    return dict(tm=tm, tf=1024, tq=1024, tk1=min(seq, 2048), tk2=tk2, unroll=unroll)


def _trunk(x3, weights):
    batch, seq, _ = x3.shape
    cfg = _tiles(seq)
    tm = cfg["tm"]
    x = x3.reshape(batch * seq, D_MODEL)

    ang_a = _rope_angles(seq, A_HEAD_DIM)
    sign = jnp.where(jnp.arange(A_HEAD_DIM) < A_HEAD_DIM // 2, -1.0, 1.0).astype(F32)
    cos_a, sin_a = jnp.cos(ang_a), jnp.sin(ang_a) * sign
    ang_b = _rope_angles(seq, ROPE_DIM)
    tab_b = jnp.concatenate([jnp.cos(ang_b), jnp.sin(ang_b)], axis=-1)

    for i in range(DEPTH):
        j = i // N_MIXERS
        if i % N_MIXERS == 0:
            q, k, v = _qkv_rope(x, weights["w_qkv_a"], weights["w_vt_a"], j, cos_a, sin_a,
                                seq, tm)
            attn = _win_attn(q, k, v, weights["sink_a"][j], batch, seq)
            w_o = weights["w_o_a"]
        else:
            q, k, v = _mla_proj(x, *weights["mla"][j], tab_b, seq, tm)
            attn = _mla_attn(q, k, v, batch, seq, cfg["tq"], cfg["tk1"], cfg["tk2"],
                             cfg["unroll"])
            w_o = weights["w_o_b"]
        g, b = weights["ln_g"], weights["ln_b"]
        x = _proj_res_ln(attn, w_o, j, x, g[i, 0:1], b[i, 0:1], tm)
        x = _mlp(x, weights["w_up"], weights["w_down"], i, g[i, 1:2], b[i, 1:2],
                 tm, cfg["tf"])
    return x.reshape(batch, seq, D_MODEL)


def kernel(x_prompt, x_sample, w_qkv_a, sink_a, w_o_a, w_dqkv_b, q_norm_b, kv_norm_b,
           w_uq_b, w_ukv_b, w_o_b, w_up, w_down, ln_g, ln_b):
    weights = dict(
        w_qkv_a=w_qkv_a.astype(BF16),
        w_vt_a=jnp.swapaxes(w_qkv_a[:, :, (A_HEADS + A_KV_HEADS) * A_HEAD_DIM:], 1, 2).astype(BF16),
        sink_a=sink_a.astype(F32),
        w_o_a=w_o_a.astype(BF16),
        mla=[_prep_mla_weights(w_dqkv_b[j], q_norm_b[j], kv_norm_b[j], w_uq_b[j], w_ukv_b[j])
             for j in range(w_dqkv_b.shape[0])],
        w_o_b=w_o_b.astype(BF16),
        w_up=w_up.astype(BF16),
        w_down=w_down.astype(BF16),
        ln_g=ln_g.astype(F32),
        ln_b=ln_b.astype(F32),
    )
    return (_trunk(x_prompt, weights), _trunk(x_sample, weights))
```

```python
import functools
import math

import jax
import jax.numpy as jnp
from jax import lax
from jax.experimental import pallas as pl
from jax.experimental.pallas import tpu as pltpu

D_MODEL = 2048
DEPTH = 4
N_MIXERS = 2
BLK = 128
A_HEADS = 16
A_KV_HEADS = 4
A_GROUP = A_HEADS // A_KV_HEADS
A_HEAD_DIM = D_MODEL // A_HEADS
WINDOW = 128
B_HEADS = 16
Q_LORA = 448
KV_LORA = 128
NOPE_DIM = 128
ROPE_DIM = 64
V_DIM = 128
D_FF = 4 * D_MODEL
ROPE_THETA = 10000.0
LN_EPS = 1e-5
RMS_EPS = 1e-6
ALPHA = (2 * DEPTH) ** 0.25

LANES = 128
MXU_DIM = 256
VMEM_LIMIT_BYTES = 56 * 1024 * 1024

Q_LORA_PAD = 512
LAT_COLS = Q_LORA_PAD + KV_LORA + 2 * ROPE_DIM
QK_SLAB = MXU_DIM
BF16_SUBLANES = 16
VT_SLAB = V_DIM + BF16_SUBLANES
assert A_HEAD_DIM == V_DIM
MLA_SCORE_SCRATCH_BYTES = 32 * 1024 * 1024

F32 = jnp.float32
BF16 = jnp.bfloat16


def _compiler_params(semantics):
    return pltpu.CompilerParams(dimension_semantics=semantics,
                                vmem_limit_bytes=VMEM_LIMIT_BYTES)


def _layer_norm_rows(y, g, b):
    mu = jnp.mean(y, axis=-1, keepdims=True)
    d = y - mu
    var = jnp.mean(d * d, axis=-1, keepdims=True)
    return d * lax.rsqrt(var + LN_EPS) * g + b


def _dot_nt(a, b):
    return lax.dot_general(a, b, (((1,), (1,)), ((), ())), preferred_element_type=F32)


def _qkv_rope_kernel(x_ref, w_ref, wvt_ref, cos_ref, sin_ref, q_ref, k_ref, vt_ref):
    xb = x_ref[...].astype(BF16)
    cos = cos_ref[...]
    sin = sin_ref[...]
    chunk = A_GROUP * A_HEAD_DIM
    q_cols = A_HEADS * A_HEAD_DIM
    k_cols = A_KV_HEADS * A_HEAD_DIM

    def rope(t):
        return t * cos + pltpu.roll(t, A_HEAD_DIM // 2, 1) * sin

    for c in range(0, q_cols + k_cols, chunk):
        acc = jnp.dot(xb, w_ref[:, c:c + chunk], preferred_element_type=F32)
        for hh in range(chunk // A_HEAD_DIM):
            y = rope(acc[:, hh * A_HEAD_DIM:(hh + 1) * A_HEAD_DIM]).astype(BF16)
            col = c + hh * A_HEAD_DIM
            if col < q_cols:
                q_ref[:, col:col + A_HEAD_DIM] = y
            else:
                k_ref[:, col - q_cols:col - q_cols + A_HEAD_DIM] = y
    vt = _dot_nt(wvt_ref[...], xb).astype(BF16)
    ones = jnp.ones((VT_SLAB - V_DIM, xb.shape[0]), BF16)
    for h in range(A_KV_HEADS):
        vt_ref[h * VT_SLAB:h * VT_SLAB + A_HEAD_DIM, :] = vt[h * A_HEAD_DIM:(h + 1) * A_HEAD_DIM]
        vt_ref[h * VT_SLAB + A_HEAD_DIM:(h + 1) * VT_SLAB, :] = ones


def _qkv_rope(x, w_stack, wvt_stack, layer, cos, sin, seq, tm):
    t = x.shape[0]
    tiles_per_seq = seq // tm
    q_cols = A_HEADS * A_HEAD_DIM
    k_cols = A_KV_HEADS * A_HEAD_DIM
    row = lambda i: (i, 0)
    pos = lambda i: (i % tiles_per_seq, 0)
    return pl.pallas_call(
        _qkv_rope_kernel,
        grid=(t // tm,),
        in_specs=[
            pl.BlockSpec((tm, D_MODEL), row),
            pl.BlockSpec((None, D_MODEL, q_cols + k_cols), lambda i: (layer, 0, 0)),
            pl.BlockSpec((None, k_cols, D_MODEL), lambda i: (layer, 0, 0)),
            pl.BlockSpec((tm, A_HEAD_DIM), pos),
            pl.BlockSpec((tm, A_HEAD_DIM), pos),
        ],
        out_specs=[
            pl.BlockSpec((tm, q_cols), row),
            pl.BlockSpec((tm, k_cols), row),
            pl.BlockSpec((A_KV_HEADS * VT_SLAB, tm), lambda i: (0, i)),
        ],
        out_shape=[
            jax.ShapeDtypeStruct((t, q_cols), BF16),
            jax.ShapeDtypeStruct((t, k_cols), BF16),
            jax.ShapeDtypeStruct((A_KV_HEADS * VT_SLAB, t), BF16),
        ],
        compiler_params=_compiler_params(("parallel",)),
        name="qkv_rope",
    )(x, w_stack, wvt_stack, cos, sin)


def _win_attn_kernel(sink_ref, q_ref, kp_ref, kc_ref, kn_ref, vp_ref, vc_ref, vn_ref,
                     o_ref, *, nb):
    n = pl.program_id(1)
    cols = A_GROUP * BLK
    span = BLK + 2 * WINDOW
    key = lax.broadcasted_iota(jnp.int32, (span, cols), 0)
    qi = lax.broadcasted_iota(jnp.int32, (span, cols), 1) & (BLK - 1)
    lo = jnp.where(n > 0, 0, WINDOW)
    hi = jnp.where(n < nb - 1, span, BLK + WINDOW)
    valid = (key >= jnp.maximum(qi, lo)) & (key <= qi + 2 * WINDOW) & (key < hi)
    bias = jnp.where(valid, 0.0, -jnp.inf).astype(F32)
    grp = lax.broadcasted_iota(jnp.int32, (1, cols), 1) // BLK
    log2e = math.log2(math.e)
    scale2 = log2e / math.sqrt(A_HEAD_DIM)

    scores = []
    for h in range(A_KV_HEADS):
        ks = slice(h * A_HEAD_DIM, (h + 1) * A_HEAD_DIM)
        qh = jnp.concatenate(
            [q_ref[:, (h * A_GROUP + g) * A_HEAD_DIM:(h * A_GROUP + g + 1) * A_HEAD_DIM]
             for g in range(A_GROUP)], axis=0)
        kh = jnp.concatenate([kp_ref[:, ks], kc_ref[:, ks], kn_ref[:, ks]], axis=0)
        scores.append(_dot_nt(kh, qh) * scale2 + bias)

    for h in range(A_KV_HEADS):
        vs = slice(h * VT_SLAB, (h + 1) * VT_SLAB)
        vth = jnp.concatenate([vp_ref[vs, :], vc_ref[vs, :], vn_ref[vs, :]], axis=1)
        s = scores[h]
        sink = jnp.full((1, cols), sink_ref[h * A_GROUP] * log2e, F32)
        for g in range(1, A_GROUP):
            sink = jnp.where(grp == g, sink_ref[h * A_GROUP + g] * log2e, sink)
        m = jnp.maximum(jnp.max(s, axis=0, keepdims=True), sink)
        p = jnp.exp2((s - m).astype(BF16))
        acc = jnp.dot(vth, p, preferred_element_type=F32)
        denom = acc[A_HEAD_DIM:A_HEAD_DIM + 1, :] + jnp.exp2(sink - m)
        o = (acc[:A_HEAD_DIM, :] * (1.0 / denom)).T
        for g in range(A_GROUP):
            c0 = (h * A_GROUP + g) * A_HEAD_DIM
            o_ref[:, c0:c0 + A_HEAD_DIM] = o[g * BLK:(g + 1) * BLK].astype(BF16)


def _win_attn(q, k, vt, sink, batch, seq):
    t = q.shape[0]
    nb = seq // BLK
    q_cols = A_HEADS * A_HEAD_DIM
    k_cols = A_KV_HEADS * A_HEAD_DIM
    cur = lambda b, n: b * nb + n
    prev = lambda b, n: b * nb + jnp.maximum(n - 1, 0)
    nxt = lambda b, n: b * nb + jnp.minimum(n + 1, nb - 1)
    k_spec = lambda blk: pl.BlockSpec((BLK, k_cols), lambda b, n: (blk(b, n), 0))
    vt_spec = lambda blk: pl.BlockSpec((A_KV_HEADS * VT_SLAB, BLK), lambda b, n: (0, blk(b, n)))
    return pl.pallas_call(
        functools.partial(_win_attn_kernel, nb=nb),
        grid=(batch, nb),
        in_specs=[
            pl.BlockSpec(memory_space=pltpu.SMEM),
            pl.BlockSpec((BLK, q_cols), lambda b, n: (cur(b, n), 0)),
            k_spec(prev), k_spec(cur), k_spec(nxt),
            vt_spec(prev), vt_spec(cur), vt_spec(nxt),
        ],
        out_specs=pl.BlockSpec((BLK, q_cols), lambda b, n: (cur(b, n), 0)),
        out_shape=jax.ShapeDtypeStruct((t, q_cols), BF16),
        compiler_params=_compiler_params(("parallel", "parallel")),
        name="win_attn",
    )(sink, q, k, k, k, vt, vt, vt)


def _proj_res_ln_kernel(a_ref, w_ref, x_ref, g_ref, b_ref, o_ref):
    h = jnp.dot(a_ref[...], w_ref[...], preferred_element_type=F32)
    o_ref[...] = _layer_norm_rows(ALPHA * x_ref[...] + h, g_ref[...], b_ref[...])


def _proj_res_ln(a, w_stack, layer, x, g, b, tm):
    t, kdim = a.shape
    row = lambda i: (i, 0)
    const = lambda i: (0, 0)
    return pl.pallas_call(
        _proj_res_ln_kernel,
        grid=(t // tm,),
        in_specs=[
            pl.BlockSpec((tm, kdim), row),
            pl.BlockSpec((None, kdim, D_MODEL), lambda i: (layer, 0, 0)),
            pl.BlockSpec((tm, D_MODEL), row),
            pl.BlockSpec((1, D_MODEL), const),
            pl.BlockSpec((1, D_MODEL), const),
        ],
        out_specs=pl.BlockSpec((tm, D_MODEL), row),
        out_shape=jax.ShapeDtypeStruct((t, D_MODEL), F32),
        compiler_params=_compiler_params(("parallel",)),
        name="proj_res_ln",
    )(a, w_stack, x, g, b)


def _mlp_kernel(x_ref, wu_ref, wd_ref, g_ref, b_ref, o_ref, xb_ref):
    j = pl.program_id(1)

    @pl.when(j == 0)
    def _():
        xb_ref[...] = x_ref[...].astype(BF16)
        o_ref[...] = jnp.zeros_like(o_ref)

    h = jnp.dot(xb_ref[...], wu_ref[...], preferred_element_type=F32)
    h = jnp.square(jnp.maximum(h, 0.0)).astype(BF16)
    o_ref[...] += jnp.dot(h, wd_ref[...], preferred_element_type=F32)

    @pl.when(j == pl.num_programs(1) - 1)
    def _():
        o_ref[...] = _layer_norm_rows(ALPHA * x_ref[...] + o_ref[...], g_ref[...], b_ref[...])


def _mlp(x, wu_stack, wd_stack, layer, g, b, tm, tf):
    t = x.shape[0]
    row = lambda i, j: (i, 0)
    const = lambda i, j: (0, 0)
    return pl.pallas_call(
        _mlp_kernel,
        grid=(t // tm, D_FF // tf),
        in_specs=[
            pl.BlockSpec((tm, D_MODEL), row),
            pl.BlockSpec((None, D_MODEL, tf), lambda i, j: (layer, 0, j)),
            pl.BlockSpec((None, tf, D_MODEL), lambda i, j: (layer, j, 0)),
            pl.BlockSpec((1, D_MODEL), const),
            pl.BlockSpec((1, D_MODEL), const),
        ],
        out_specs=pl.BlockSpec((tm, D_MODEL), row),
        out_shape=jax.ShapeDtypeStruct((t, D_MODEL), F32),
        scratch_shapes=[pltpu.VMEM((tm, D_MODEL), BF16)],
        compiler_params=_compiler_params(("parallel", "arbitrary")),
        name="mlp",
    )(x, wu_stack, wd_stack, g, b)


def _mla_proj_kernel(x_ref, wd_ref, gq_ref, gkv_ref, wuq_ref, wuk_ref, wuvt_ref, tab_ref,
                     q_ref, k_ref, vt_ref):
    xb = x_ref[...].astype(BF16)
    tab = tab_ref[...]
    lane = lax.broadcasted_iota(jnp.int32, tab.shape, 1)

    def rope_pair(y):
        t = y * tab
        return jnp.where(lane < ROPE_DIM, t + pltpu.roll(t, ROPE_DIM, 1), 0.0).astype(BF16)

    lat = jnp.dot(xb, wd_ref[...], preferred_element_type=F32)
    cq = lat[:, :Q_LORA_PAD]
    cq = cq * lax.rsqrt(jnp.sum(cq * cq, axis=-1, keepdims=True) * (1.0 / Q_LORA) + RMS_EPS)
    cq = (cq * gq_ref[...]).astype(BF16)
    ckv = lat[:, Q_LORA_PAD:Q_LORA_PAD + KV_LORA]
    ckv = ckv * lax.rsqrt(jnp.mean(ckv * ckv, axis=-1, keepdims=True) + RMS_EPS)
    ckv = (ckv * gkv_ref[...]).astype(BF16)
    kr = rope_pair(lat[:, Q_LORA_PAD + KV_LORA:])

    heads_per_chunk = 2
    qchunk = heads_per_chunk * QK_SLAB
    for c in range(0, B_HEADS * QK_SLAB, qchunk):
        qq = jnp.dot(cq, wuq_ref[:, c:c + qchunk], preferred_element_type=F32)
        for hh in range(heads_per_chunk):
            o = hh * QK_SLAB
            q_ref[:, c + o:c + o + NOPE_DIM] = qq[:, o:o + NOPE_DIM].astype(BF16)
            q_ref[:, c + o + NOPE_DIM:c + o + QK_SLAB] = rope_pair(qq[:, o + NOPE_DIM:o + QK_SLAB])

    kchunk = 4 * NOPE_DIM
    for c in range(0, B_HEADS * NOPE_DIM, kchunk):
        kn = jnp.dot(ckv, wuk_ref[:, c:c + kchunk], preferred_element_type=F32)
        for hh in range(kchunk // NOPE_DIM):
            h = c // NOPE_DIM + hh
            k_ref[:, h * QK_SLAB:h * QK_SLAB + NOPE_DIM] = (
                kn[:, hh * NOPE_DIM:(hh + 1) * NOPE_DIM].astype(BF16))
            k_ref[:, h * QK_SLAB + NOPE_DIM:(h + 1) * QK_SLAB] = kr
    ones = jnp.ones((VT_SLAB - V_DIM, ckv.shape[0]), BF16)
    for c in range(0, B_HEADS * V_DIM, kchunk):
        vt = _dot_nt(wuvt_ref[c:c + kchunk, :], ckv).astype(BF16)
        for hh in range(kchunk // V_DIM):
            r0 = (c // V_DIM + hh) * VT_SLAB
            vt_ref[r0:r0 + V_DIM, :] = vt[hh * V_DIM:(hh + 1) * V_DIM]
            vt_ref[r0 + V_DIM:r0 + VT_SLAB, :] = ones


def _mla_proj(x, wd, gq, gkv, wuq, wuk, wuvt, tab, seq, tm):
    t = x.shape[0]
    tiles_per_seq = seq // tm
    row = lambda i: (i, 0)
    const = lambda i: (0, 0)
    full = lambda a: pl.BlockSpec(a.shape, const)
    return pl.pallas_call(
        _mla_proj_kernel,
        grid=(t // tm,),
        in_specs=[
            pl.BlockSpec((tm, D_MODEL), row),
            full(wd), full(gq), full(gkv), full(wuq), full(wuk), full(wuvt),
            pl.BlockSpec((tm, LANES), lambda i: (i % tiles_per_seq, 0)),
        ],
        out_specs=[
            pl.BlockSpec((tm, B_HEADS * QK_SLAB), row),
            pl.BlockSpec((tm, B_HEADS * QK_SLAB), row),
            pl.BlockSpec((B_HEADS * VT_SLAB, tm), lambda i: (0, i)),
        ],
        out_shape=[
            jax.ShapeDtypeStruct((t, B_HEADS * QK_SLAB), BF16),
            jax.ShapeDtypeStruct((t, B_HEADS * QK_SLAB), BF16),
            jax.ShapeDtypeStruct((B_HEADS * VT_SLAB, t), BF16),
        ],
        compiler_params=_compiler_params(("parallel",)),
        name="mla_proj",
    )(x, wd, gq, gkv, wuq, wuk, wuvt, tab)


F32_SUBLANES = 8
MLA_EXP2_SCALE = math.log2(math.e) / math.sqrt(NOPE_DIM + ROPE_DIM)


def _mla_attn_kernel(q_ref, k_ref, vt_ref, o_ref, s_ref, acc_ref, *, tk1, tk2, unroll):
    seq = k_ref.shape[0]
    tq = q_ref.shape[0]
    sub = F32_SUBLANES
    q = q_ref[...]

    m8 = jnp.full((sub, tq), -jnp.inf, F32)
    for c0 in range(0, seq, tk1):
        s = _dot_nt(k_ref[c0:c0 + tk1, :], q) * MLA_EXP2_SCALE
        s_ref[c0:c0 + tk1, :] = s
        m8 = jnp.maximum(m8, jnp.max(s.reshape(tk1 // sub, sub, tq), axis=0))
    m = jnp.max(m8, axis=0, keepdims=True)

    acc_ref[...] = jnp.zeros_like(acc_ref)

    def body(c, carry):
        start = pl.multiple_of(c * tk2, tk2)
        p = jnp.exp2((s_ref[pl.ds(start, tk2), :] - m).astype(BF16))
        acc_ref[...] += jnp.dot(vt_ref[:, pl.ds(start, tk2)], p, preferred_element_type=F32)
        return carry

    lax.fori_loop(0, seq // tk2, body, 0, unroll=unroll)
    inv_l = 1.0 / acc_ref[V_DIM:V_DIM + 1, :]
    o_ref[...] = (acc_ref[:V_DIM, :] * inv_l).T.astype(BF16)


def _mla_attn(q, k, vt, batch, seq, tq, tk1, tk2, unroll):
    t = q.shape[0]
    nq = seq // tq
    return pl.pallas_call(
        functools.partial(_mla_attn_kernel, tk1=tk1, tk2=tk2, unroll=unroll),
        grid=(batch, B_HEADS, nq),
        in_specs=[
            pl.BlockSpec((tq, QK_SLAB), lambda b, h, i: (b * nq + i, h)),
            pl.BlockSpec((seq, QK_SLAB), lambda b, h, i: (b, h)),
            pl.BlockSpec((VT_SLAB, seq), lambda b, h, i: (h, b)),
        ],
        out_specs=pl.BlockSpec((tq, V_DIM), lambda b, h, i: (b * nq + i, h)),
        out_shape=jax.ShapeDtypeStruct((t, B_HEADS * V_DIM), BF16),
        scratch_shapes=[
            pltpu.VMEM((seq, tq), F32),
            pltpu.VMEM((VT_SLAB, tq), F32),
        ],
        compiler_params=_compiler_params(("parallel", "parallel", "arbitrary")),
        name="mla_attn",
    )(q, k, vt)


def _rope_angles(seq, dim):
    inv = 1.0 / (ROPE_THETA ** (jnp.arange(0, dim, 2, dtype=F32) / dim))
    ang = jnp.arange(seq, dtype=F32)[:, None] * inv[None, :]
    return jnp.concatenate([ang, ang], axis=-1)


def _rot_half_cols(w):
    half = w.shape[-1] // 2
    return jnp.concatenate([-w[:, half:], w[:, :half]], axis=-1)


def _prep_mla_weights(w_dqkv, q_norm, kv_norm, w_uq, w_ukv):
    wq = w_dqkv[:, :Q_LORA]
    wkv = w_dqkv[:, Q_LORA:Q_LORA + KV_LORA]
    wr = w_dqkv[:, Q_LORA + KV_LORA:]
    zpad = jnp.zeros((D_MODEL, Q_LORA_PAD - Q_LORA), w_dqkv.dtype)
    wd = jnp.concatenate([wq, zpad, wkv, wr, _rot_half_cols(wr)], axis=1).astype(BF16)
    gq = jnp.pad(q_norm, (0, Q_LORA_PAD - Q_LORA)).reshape(1, Q_LORA_PAD)
    gkv = kv_norm.reshape(1, KV_LORA)
    uq = w_uq.reshape(Q_LORA, B_HEADS, NOPE_DIM + ROPE_DIM)
    uq_r = uq[:, :, NOPE_DIM:]
    uq_rot = jnp.concatenate([-uq_r[..., ROPE_DIM // 2:], uq_r[..., :ROPE_DIM // 2]], axis=-1)
    wuq = jnp.concatenate([uq, uq_rot], axis=-1).reshape(Q_LORA, B_HEADS * QK_SLAB)
    wuq = jnp.pad(wuq, ((0, Q_LORA_PAD - Q_LORA), (0, 0))).astype(BF16)
    ukv = w_ukv.reshape(KV_LORA, B_HEADS, NOPE_DIM + V_DIM)
    wuk = ukv[:, :, :NOPE_DIM].reshape(KV_LORA, B_HEADS * NOPE_DIM).astype(BF16)
    wuvt = ukv[:, :, NOPE_DIM:].reshape(KV_LORA, B_HEADS * V_DIM).T.astype(BF16)
    return wd, gq, gkv, wuq, wuk, wuvt


def _tiles(seq):
    tm = 512
    assert seq % tm == 0 and seq % BLK == 0
    tk2 = 512
    unroll = max(1, min(8, seq // tk2 // 2))
    tq = min(seq, MLA_SCORE_SCRATCH_BYTES // (4 * seq))
    assert seq % tq == 0 and tq % MXU_DIM == 0
    return dict(tm=tm, tf=1024, tq=tq, tk1=min(seq, 2048), tk2=tk2, unroll=unroll)


def _trunk(x3, weights):
    batch, seq, _ = x3.shape
    cfg = _tiles(seq)
    tm = cfg["tm"]
    x = x3.reshape(batch * seq, D_MODEL)

    ang_a = _rope_angles(seq, A_HEAD_DIM)
    sign = jnp.where(jnp.arange(A_HEAD_DIM) < A_HEAD_DIM // 2, -1.0, 1.0).astype(F32)
    cos_a, sin_a = jnp.cos(ang_a), jnp.sin(ang_a) * sign
    ang_b = _rope_angles(seq, ROPE_DIM)
    tab_b = jnp.concatenate([jnp.cos(ang_b), jnp.sin(ang_b)], axis=-1)

    for i in range(DEPTH):
        j = i // N_MIXERS
        if i % N_MIXERS == 0:
            q, k, v = _qkv_rope(x, weights["w_qkv_a"], weights["w_vt_a"], j, cos_a, sin_a,
                                seq, tm)
            attn = _win_attn(q, k, v, weights["sink_a"][j], batch, seq)
            w_o = weights["w_o_a"]
        else:
            q, k, v = _mla_proj(x, *weights["mla"][j], tab_b, seq, tm)
            attn = _mla_attn(q, k, v, batch, seq, cfg["tq"], cfg["tk1"], cfg["tk2"],
                             cfg["unroll"])
            w_o = weights["w_o_b"]
        g, b = weights["ln_g"], weights["ln_b"]
        x = _proj_res_ln(attn, w_o, j, x, g[i, 0:1], b[i, 0:1], tm)
        x = _mlp(x, weights["w_up"], weights["w_down"], i, g[i, 1:2], b[i, 1:2],
                 tm, cfg["tf"])
    return x.reshape(batch, seq, D_MODEL)


def kernel(x_prompt, x_sample, w_qkv_a, sink_a, w_o_a, w_dqkv_b, q_norm_b, kv_norm_b,
           w_uq_b, w_ukv_b, w_o_b, w_up, w_down, ln_g, ln_b):
    weights = dict(
        w_qkv_a=w_qkv_a.astype(BF16),
        w_vt_a=jnp.swapaxes(w_qkv_a[:, :, (A_HEADS + A_KV_HEADS) * A_HEAD_DIM:], 1, 2).astype(BF16),
        sink_a=sink_a.astype(F32),
        w_o_a=w_o_a.astype(BF16),
        mla=[_prep_mla_weights(w_dqkv_b[j], q_norm_b[j], kv_norm_b[j], w_uq_b[j], w_ukv_b[j])
             for j in range(w_dqkv_b.shape[0])],
        w_o_b=w_o_b.astype(BF16),
        w_up=w_up.astype(BF16),
        w_down=w_down.astype(BF16),
        ln_g=ln_g.astype(F32),
        ln_b=ln_b.astype(F32),
    )
    return (_trunk(x_prompt, weights), _trunk(x_sample, weights))
```

```python
import functools
import math

import jax
import jax.numpy as jnp
from jax import lax
from jax.experimental import pallas as pl
from jax.experimental.pallas import tpu as pltpu

D_MODEL = 2048
DEPTH = 4
N_MIXERS = 2
BLK = 128
A_HEADS = 16
A_KV_HEADS = 4
A_GROUP = A_HEADS // A_KV_HEADS
A_HEAD_DIM = D_MODEL // A_HEADS
WINDOW = 128
B_HEADS = 16
Q_LORA = 448
KV_LORA = 128
NOPE_DIM = 128
ROPE_DIM = 64
V_DIM = 128
D_FF = 4 * D_MODEL
ROPE_THETA = 10000.0
LN_EPS = 1e-5
RMS_EPS = 1e-6
ALPHA = (2 * DEPTH) ** 0.25

LANES = 128
MXU_DIM = 256
VMEM_LIMIT_BYTES = 56 * 1024 * 1024

Q_LORA_PAD = 512
LAT_COLS = Q_LORA_PAD + KV_LORA + 2 * ROPE_DIM
QK_SLAB = MXU_DIM
BF16_SUBLANES = 16
VT_SLAB = V_DIM + BF16_SUBLANES
assert A_HEAD_DIM == V_DIM
MLA_SCORE_SCRATCH_BYTES = 32 * 1024 * 1024

F32 = jnp.float32
BF16 = jnp.bfloat16


def _compiler_params(semantics):
    return pltpu.CompilerParams(dimension_semantics=semantics,
                                vmem_limit_bytes=VMEM_LIMIT_BYTES)


def _layer_norm_rows(y, g, b):
    mu = jnp.mean(y, axis=-1, keepdims=True)
    d = y - mu
    var = jnp.mean(d * d, axis=-1, keepdims=True)
    return d * lax.rsqrt(var + LN_EPS) * g + b


def _dot_nt(a, b):
    return lax.dot_general(a, b, (((1,), (1,)), ((), ())), preferred_element_type=F32)


def _qkv_rope_kernel(x_ref, w_ref, wvt_ref, cos_ref, sin_ref, q_ref, k_ref, vt_ref):
    xb = x_ref[...].astype(BF16)
    cos = cos_ref[...]
    sin = sin_ref[...]
    chunk = A_GROUP * A_HEAD_DIM
    q_cols = A_HEADS * A_HEAD_DIM
    k_cols = A_KV_HEADS * A_HEAD_DIM

    def rope(t):
        return t * cos + pltpu.roll(t, A_HEAD_DIM // 2, 1) * sin

    for c in range(0, q_cols + k_cols, chunk):
        acc = jnp.dot(xb, w_ref[:, c:c + chunk], preferred_element_type=F32)
        for hh in range(chunk // A_HEAD_DIM):
            y = rope(acc[:, hh * A_HEAD_DIM:(hh + 1) * A_HEAD_DIM]).astype(BF16)
            col = c + hh * A_HEAD_DIM
            if col < q_cols:
                q_ref[:, col:col + A_HEAD_DIM] = y
            else:
                k_ref[:, col - q_cols:col - q_cols + A_HEAD_DIM] = y
    vt = _dot_nt(wvt_ref[...], xb).astype(BF16)
    ones = jnp.ones((VT_SLAB - V_DIM, xb.shape[0]), BF16)
    for h in range(A_KV_HEADS):
        vt_ref[h * VT_SLAB:h * VT_SLAB + A_HEAD_DIM, :] = vt[h * A_HEAD_DIM:(h + 1) * A_HEAD_DIM]
        vt_ref[h * VT_SLAB + A_HEAD_DIM:(h + 1) * VT_SLAB, :] = ones


def _qkv_rope(x, w_stack, wvt_stack, layer, cos, sin, seq, tm):
    t = x.shape[0]
    tiles_per_seq = seq // tm
    q_cols = A_HEADS * A_HEAD_DIM
    k_cols = A_KV_HEADS * A_HEAD_DIM
    row = lambda i: (i, 0)
    pos = lambda i: (i % tiles_per_seq, 0)
    return pl.pallas_call(
        _qkv_rope_kernel,
        grid=(t // tm,),
        in_specs=[
            pl.BlockSpec((tm, D_MODEL), row),
            pl.BlockSpec((None, D_MODEL, q_cols + k_cols), lambda i: (layer, 0, 0)),
            pl.BlockSpec((None, k_cols, D_MODEL), lambda i: (layer, 0, 0)),
            pl.BlockSpec((tm, A_HEAD_DIM), pos),
            pl.BlockSpec((tm, A_HEAD_DIM), pos),
        ],
        out_specs=[
            pl.BlockSpec((tm, q_cols), row),
            pl.BlockSpec((tm, k_cols), row),
            pl.BlockSpec((A_KV_HEADS * VT_SLAB, tm), lambda i: (0, i)),
        ],
        out_shape=[
            jax.ShapeDtypeStruct((t, q_cols), BF16),
            jax.ShapeDtypeStruct((t, k_cols), BF16),
            jax.ShapeDtypeStruct((A_KV_HEADS * VT_SLAB, t), BF16),
        ],
        compiler_params=_compiler_params(("parallel",)),
        name="qkv_rope",
    )(x, w_stack, wvt_stack, cos, sin)


def _win_attn_kernel(sink_ref, q_ref, kp_ref, kc_ref, kn_ref, vp_ref, vc_ref, vn_ref,
                     o_ref, *, nb):
    n = pl.program_id(1)
    cols = A_GROUP * BLK
    span = BLK + 2 * WINDOW
    key = lax.broadcasted_iota(jnp.int32, (span, cols), 0)
    qi = lax.broadcasted_iota(jnp.int32, (span, cols), 1) & (BLK - 1)
    lo = jnp.where(n > 0, 0, WINDOW)
    hi = jnp.where(n < nb - 1, span, BLK + WINDOW)
    valid = (key >= jnp.maximum(qi, lo)) & (key <= qi + 2 * WINDOW) & (key < hi)
    bias = jnp.where(valid, 0.0, -jnp.inf).astype(F32)
    grp = lax.broadcasted_iota(jnp.int32, (1, cols), 1) // BLK
    log2e = math.log2(math.e)
    scale2 = log2e / math.sqrt(A_HEAD_DIM)

    scores = []
    for h in range(A_KV_HEADS):
        ks = slice(h * A_HEAD_DIM, (h + 1) * A_HEAD_DIM)
        qh = jnp.concatenate(
            [q_ref[:, (h * A_GROUP + g) * A_HEAD_DIM:(h * A_GROUP + g + 1) * A_HEAD_DIM]
             for g in range(A_GROUP)], axis=0)
        kh = jnp.concatenate([kp_ref[:, ks], kc_ref[:, ks], kn_ref[:, ks]], axis=0)
        scores.append(_dot_nt(kh, qh) * scale2 + bias)

    for h in range(A_KV_HEADS):
        vs = slice(h * VT_SLAB, (h + 1) * VT_SLAB)
        vth = jnp.concatenate([vp_ref[vs, :], vc_ref[vs, :], vn_ref[vs, :]], axis=1)
        s = scores[h]
        sink = jnp.full((1, cols), sink_ref[h * A_GROUP] * log2e, F32)
        for g in range(1, A_GROUP):
            sink = jnp.where(grp == g, sink_ref[h * A_GROUP + g] * log2e, sink)
        m = jnp.maximum(jnp.max(s, axis=0, keepdims=True), sink)
        p = jnp.exp2((s - m).astype(BF16))
        acc = jnp.dot(vth, p, preferred_element_type=F32)
        denom = acc[A_HEAD_DIM:A_HEAD_DIM + 1, :] + jnp.exp2(sink - m)
        o = (acc[:A_HEAD_DIM, :] * (1.0 / denom)).T
        for g in range(A_GROUP):
            c0 = (h * A_GROUP + g) * A_HEAD_DIM
            o_ref[:, c0:c0 + A_HEAD_DIM] = o[g * BLK:(g + 1) * BLK].astype(BF16)


def _win_attn(q, k, vt, sink, batch, seq):
    t = q.shape[0]
    nb = seq // BLK
    q_cols = A_HEADS * A_HEAD_DIM
    k_cols = A_KV_HEADS * A_HEAD_DIM
    cur = lambda b, n: b * nb + n
    prev = lambda b, n: b * nb + jnp.maximum(n - 1, 0)
    nxt = lambda b, n: b * nb + jnp.minimum(n + 1, nb - 1)
    k_spec = lambda blk: pl.BlockSpec((BLK, k_cols), lambda b, n: (blk(b, n), 0))
    vt_spec = lambda blk: pl.BlockSpec((A_KV_HEADS * VT_SLAB, BLK), lambda b, n: (0, blk(b, n)))
    return pl.pallas_call(
        functools.partial(_win_attn_kernel, nb=nb),
        grid=(batch, nb),
        in_specs=[
            pl.BlockSpec(memory_space=pltpu.SMEM),
            pl.BlockSpec((BLK, q_cols), lambda b, n: (cur(b, n), 0)),
            k_spec(prev), k_spec(cur), k_spec(nxt),
            vt_spec(prev), vt_spec(cur), vt_spec(nxt),
        ],
        out_specs=pl.BlockSpec((BLK, q_cols), lambda b, n: (cur(b, n), 0)),
        out_shape=jax.ShapeDtypeStruct((t, q_cols), BF16),
        compiler_params=_compiler_params(("parallel", "parallel")),
        name="win_attn",
    )(sink, q, k, k, k, vt, vt, vt)


def _proj_res_ln_kernel(a_ref, w_ref, x_ref, g_ref, b_ref, o_ref):
    h = jnp.dot(a_ref[...], w_ref[...], preferred_element_type=F32)
    o_ref[...] = _layer_norm_rows(ALPHA * x_ref[...] + h, g_ref[...], b_ref[...])


def _proj_res_ln(a, w_stack, layer, x, g, b, tm):
    t, kdim = a.shape
    row = lambda i: (i, 0)
    const = lambda i: (0, 0)
    return pl.pallas_call(
        _proj_res_ln_kernel,
        grid=(t // tm,),
        in_specs=[
            pl.BlockSpec((tm, kdim), row),
            pl.BlockSpec((None, kdim, D_MODEL), lambda i: (layer, 0, 0)),
            pl.BlockSpec((tm, D_MODEL), row),
            pl.BlockSpec((1, D_MODEL), const),
            pl.BlockSpec((1, D_MODEL), const),
        ],
        out_specs=pl.BlockSpec((tm, D_MODEL), row),
        out_shape=jax.ShapeDtypeStruct((t, D_MODEL), F32),
        compiler_params=_compiler_params(("parallel",)),
        name="proj_res_ln",
    )(a, w_stack, x, g, b)


def _mlp_kernel(x_ref, wu_ref, wd_ref, g_ref, b_ref, o_ref, xb_ref):
    j = pl.program_id(1)

    def down_of_hidden(xb):
        h = jnp.dot(xb, wu_ref[...], preferred_element_type=F32)
        h = jnp.square(jnp.maximum(h, 0.0)).astype(BF16)
        return jnp.dot(h, wd_ref[...], preferred_element_type=F32)

    @pl.when(j == 0)
    def _():
        xb = x_ref[...].astype(BF16)
        xb_ref[...] = xb
        o_ref[...] = down_of_hidden(xb)

    @pl.when(j > 0)
    def _():
        o_ref[...] += down_of_hidden(xb_ref[...])

    @pl.when(j == pl.num_programs(1) - 1)
    def _():
        o_ref[...] = _layer_norm_rows(ALPHA * x_ref[...] + o_ref[...], g_ref[...], b_ref[...])


def _mlp(x, wu_stack, wd_stack, layer, g, b, tm, tf):
    t = x.shape[0]
    row = lambda i, j: (i, 0)
    const = lambda i, j: (0, 0)
    return pl.pallas_call(
        _mlp_kernel,
        grid=(t // tm, D_FF // tf),
        in_specs=[
            pl.BlockSpec((tm, D_MODEL), row),
            pl.BlockSpec((None, D_MODEL, tf), lambda i, j: (layer, 0, j)),
            pl.BlockSpec((None, tf, D_MODEL), lambda i, j: (layer, j, 0)),
            pl.BlockSpec((1, D_MODEL), const),
            pl.BlockSpec((1, D_MODEL), const),
        ],
        out_specs=pl.BlockSpec((tm, D_MODEL), row),
        out_shape=jax.ShapeDtypeStruct((t, D_MODEL), F32),
        scratch_shapes=[pltpu.VMEM((tm, D_MODEL), BF16)],
        compiler_params=_compiler_params(("parallel", "arbitrary")),
        name="mlp",
    )(x, wu_stack, wd_stack, g, b)


def _mla_proj_kernel(x_ref, wd_ref, gq_ref, gkv_ref, wuq_ref, wuk_ref, wuvt_ref, tab_ref,
                     q_ref, k_ref, vt_ref):
    xb = x_ref[...].astype(BF16)
    tab = tab_ref[...]
    lane = lax.broadcasted_iota(jnp.int32, tab.shape, 1)

    def rope_pair(y):
        t = y * tab
        return jnp.where(lane < ROPE_DIM, t + pltpu.roll(t, ROPE_DIM, 1), 0.0).astype(BF16)

    lat = jnp.dot(xb, wd_ref[...], preferred_element_type=F32)
    cq = lat[:, :Q_LORA_PAD]
    cq = cq * lax.rsqrt(jnp.sum(cq * cq, axis=-1, keepdims=True) * (1.0 / Q_LORA) + RMS_EPS)
    cq = (cq * gq_ref[...]).astype(BF16)
    ckv = lat[:, Q_LORA_PAD:Q_LORA_PAD + KV_LORA]
    ckv = ckv * lax.rsqrt(jnp.mean(ckv * ckv, axis=-1, keepdims=True) + RMS_EPS)
    ckv = (ckv * gkv_ref[...]).astype(BF16)
    kr = rope_pair(lat[:, Q_LORA_PAD + KV_LORA:])

    heads_per_chunk = 2
    qchunk = heads_per_chunk * QK_SLAB
    for c in range(0, B_HEADS * QK_SLAB, qchunk):
        qq = jnp.dot(cq, wuq_ref[:, c:c + qchunk], preferred_element_type=F32)
        for hh in range(heads_per_chunk):
            o = hh * QK_SLAB
            q_ref[:, c + o:c + o + NOPE_DIM] = qq[:, o:o + NOPE_DIM].astype(BF16)
            q_ref[:, c + o + NOPE_DIM:c + o + QK_SLAB] = rope_pair(qq[:, o + NOPE_DIM:o + QK_SLAB])

    kchunk = 4 * NOPE_DIM
    for c in range(0, B_HEADS * NOPE_DIM, kchunk):
        kn = jnp.dot(ckv, wuk_ref[:, c:c + kchunk], preferred_element_type=F32)
        for hh in range(kchunk // NOPE_DIM):
            h = c // NOPE_DIM + hh
            k_ref[:, h * QK_SLAB:h * QK_SLAB + NOPE_DIM] = (
                kn[:, hh * NOPE_DIM:(hh + 1) * NOPE_DIM].astype(BF16))
            k_ref[:, h * QK_SLAB + NOPE_DIM:(h + 1) * QK_SLAB] = kr
    ones = jnp.ones((VT_SLAB - V_DIM, ckv.shape[0]), BF16)
    for c in range(0, B_HEADS * V_DIM, kchunk):
        vt = _dot_nt(wuvt_ref[c:c + kchunk, :], ckv).astype(BF16)
        for hh in range(kchunk // V_DIM):
            r0 = (c // V_DIM + hh) * VT_SLAB
            vt_ref[r0:r0 + V_DIM, :] = vt[hh * V_DIM:(hh + 1) * V_DIM]
            vt_ref[r0 + V_DIM:r0 + VT_SLAB, :] = ones


def _mla_proj(x, wd, gq, gkv, wuq, wuk, wuvt, tab, seq, tm):
    t = x.shape[0]
    tiles_per_seq = seq // tm
    row = lambda i: (i, 0)
    const = lambda i: (0, 0)
    full = lambda a: pl.BlockSpec(a.shape, const)
    return pl.pallas_call(
        _mla_proj_kernel,
        grid=(t // tm,),
        in_specs=[
            pl.BlockSpec((tm, D_MODEL), row),
            full(wd), full(gq), full(gkv), full(wuq), full(wuk), full(wuvt),
            pl.BlockSpec((tm, LANES), lambda i: (i % tiles_per_seq, 0)),
        ],
        out_specs=[
            pl.BlockSpec((tm, B_HEADS * QK_SLAB), row),
            pl.BlockSpec((tm, B_HEADS * QK_SLAB), row),
            pl.BlockSpec((B_HEADS * VT_SLAB, tm), lambda i: (0, i)),
        ],
        out_shape=[
            jax.ShapeDtypeStruct((t, B_HEADS * QK_SLAB), BF16),
            jax.ShapeDtypeStruct((t, B_HEADS * QK_SLAB), BF16),
            jax.ShapeDtypeStruct((B_HEADS * VT_SLAB, t), BF16),
        ],
        compiler_params=_compiler_params(("parallel",)),
        name="mla_proj",
    )(x, wd, gq, gkv, wuq, wuk, wuvt, tab)


F32_SUBLANES = 8
MLA_EXP2_SCALE = math.log2(math.e) / math.sqrt(NOPE_DIM + ROPE_DIM)


def _mla_attn_kernel(q_ref, k_ref, vt_ref, o_ref, s_ref, acc_ref, *, tk1, tk2, unroll):
    seq = k_ref.shape[0]
    tq = q_ref.shape[0]
    sub = F32_SUBLANES
    q = q_ref[...]

    m8 = jnp.full((sub, tq), -jnp.inf, F32)
    for c0 in range(0, seq, tk1):
        s = _dot_nt(k_ref[c0:c0 + tk1, :], q) * MLA_EXP2_SCALE
        s_ref[c0:c0 + tk1, :] = s
        m8 = jnp.maximum(m8, jnp.max(s.reshape(tk1 // sub, sub, tq), axis=0))
    m = jnp.max(m8, axis=0, keepdims=True)

    acc_ref[...] = jnp.zeros_like(acc_ref)

    def body(c, carry):
        start = pl.multiple_of(c * tk2, tk2)
        p = jnp.exp2((s_ref[pl.ds(start, tk2), :] - m).astype(BF16))
        acc_ref[...] += jnp.dot(vt_ref[:, pl.ds(start, tk2)], p, preferred_element_type=F32)
        return carry

    lax.fori_loop(0, seq // tk2, body, 0, unroll=unroll)
    inv_l = 1.0 / acc_ref[V_DIM:V_DIM + 1, :]
    o_ref[...] = (acc_ref[:V_DIM, :] * inv_l).T.astype(BF16)


def _mla_attn(q, k, vt, batch, seq, tq, tk1, tk2, unroll):
    t = q.shape[0]
    nq = seq // tq
    return pl.pallas_call(
        functools.partial(_mla_attn_kernel, tk1=tk1, tk2=tk2, unroll=unroll),
        grid=(batch, B_HEADS, nq),
        in_specs=[
            pl.BlockSpec((tq, QK_SLAB), lambda b, h, i: (b * nq + i, h)),
            pl.BlockSpec((seq, QK_SLAB), lambda b, h, i: (b, h)),
            pl.BlockSpec((VT_SLAB, seq), lambda b, h, i: (h, b)),
        ],
        out_specs=pl.BlockSpec((tq, V_DIM), lambda b, h, i: (b * nq + i, h)),
        out_shape=jax.ShapeDtypeStruct((t, B_HEADS * V_DIM), BF16),
        scratch_shapes=[
            pltpu.VMEM((seq, tq), F32),
            pltpu.VMEM((VT_SLAB, tq), F32),
        ],
        compiler_params=_compiler_params(("parallel", "parallel", "arbitrary")),
        name="mla_attn",
    )(q, k, vt)


def _rope_angles(seq, dim):
    inv = 1.0 / (ROPE_THETA ** (jnp.arange(0, dim, 2, dtype=F32) / dim))
    ang = jnp.arange(seq, dtype=F32)[:, None] * inv[None, :]
    return jnp.concatenate([ang, ang], axis=-1)


def _rot_half_cols(w):
    half = w.shape[-1] // 2
    return jnp.concatenate([-w[:, half:], w[:, :half]], axis=-1)


def _prep_mla_weights(w_dqkv, q_norm, kv_norm, w_uq, w_ukv):
    wq = w_dqkv[:, :Q_LORA]
    wkv = w_dqkv[:, Q_LORA:Q_LORA + KV_LORA]
    wr = w_dqkv[:, Q_LORA + KV_LORA:]
    zpad = jnp.zeros((D_MODEL, Q_LORA_PAD - Q_LORA), w_dqkv.dtype)
    wd = jnp.concatenate([wq, zpad, wkv, wr, _rot_half_cols(wr)], axis=1).astype(BF16)
    gq = jnp.pad(q_norm, (0, Q_LORA_PAD - Q_LORA)).reshape(1, Q_LORA_PAD)
    gkv = kv_norm.reshape(1, KV_LORA)
    uq = w_uq.reshape(Q_LORA, B_HEADS, NOPE_DIM + ROPE_DIM)
    uq_r = uq[:, :, NOPE_DIM:]
    uq_rot = jnp.concatenate([-uq_r[..., ROPE_DIM // 2:], uq_r[..., :ROPE_DIM // 2]], axis=-1)
    wuq = jnp.concatenate([uq, uq_rot], axis=-1).reshape(Q_LORA, B_HEADS * QK_SLAB)
    wuq = jnp.pad(wuq, ((0, Q_LORA_PAD - Q_LORA), (0, 0))).astype(BF16)
    ukv = w_ukv.reshape(KV_LORA, B_HEADS, NOPE_DIM + V_DIM)
    wuk = ukv[:, :, :NOPE_DIM].reshape(KV_LORA, B_HEADS * NOPE_DIM).astype(BF16)
    wuvt = ukv[:, :, NOPE_DIM:].reshape(KV_LORA, B_HEADS * V_DIM).T.astype(BF16)
    return wd, gq, gkv, wuq, wuk, wuvt


def _tiles(seq):
    tm = 512
    assert seq % tm == 0 and seq % BLK == 0
    tk2 = 512
    unroll = max(1, min(8, seq // tk2 // 2))
    tq = min(seq, MLA_SCORE_SCRATCH_BYTES // (4 * seq))
    assert seq % tq == 0 and tq % MXU_DIM == 0
    return dict(tm=tm, tf=1024, tq=tq, tk1=min(seq, 2048), tk2=tk2, unroll=unroll)


def _trunk(x3, weights):
    batch, seq, _ = x3.shape
    cfg = _tiles(seq)
    tm = cfg["tm"]
    x = x3.reshape(batch * seq, D_MODEL)

    ang_a = _rope_angles(seq, A_HEAD_DIM)
    sign = jnp.where(jnp.arange(A_HEAD_DIM) < A_HEAD_DIM // 2, -1.0, 1.0).astype(F32)
    cos_a, sin_a = jnp.cos(ang_a), jnp.sin(ang_a) * sign
    ang_b = _rope_angles(seq, ROPE_DIM)
    tab_b = jnp.concatenate([jnp.cos(ang_b), jnp.sin(ang_b)], axis=-1)

    for i in range(DEPTH):
        j = i // N_MIXERS
        if i % N_MIXERS == 0:
            q, k, v = _qkv_rope(x, weights["w_qkv_a"], weights["w_vt_a"], j, cos_a, sin_a,
                                seq, tm)
            attn = _win_attn(q, k, v, weights["sink_a"][j], batch, seq)
            w_o = weights["w_o_a"]
        else:
            q, k, v = _mla_proj(x, *weights["mla"][j], tab_b, seq, tm)
            attn = _mla_attn(q, k, v, batch, seq, cfg["tq"], cfg["tk1"], cfg["tk2"],
                             cfg["unroll"])
            w_o = weights["w_o_b"]
        g, b = weights["ln_g"], weights["ln_b"]
        x = _proj_res_ln(attn, w_o, j, x, g[i, 0:1], b[i, 0:1], tm)
        x = _mlp(x, weights["w_up"], weights["w_down"], i, g[i, 1:2], b[i, 1:2],
                 tm, cfg["tf"])
    return x.reshape(batch, seq, D_MODEL)


def kernel(x_prompt, x_sample, w_qkv_a, sink_a, w_o_a, w_dqkv_b, q_norm_b, kv_norm_b,
           w_uq_b, w_ukv_b, w_o_b, w_up, w_down, ln_g, ln_b):
    weights = dict(
        w_qkv_a=w_qkv_a.astype(BF16),
        w_vt_a=jnp.swapaxes(w_qkv_a[:, :, (A_HEADS + A_KV_HEADS) * A_HEAD_DIM:], 1, 2).astype(BF16),
        sink_a=sink_a.astype(F32),
        w_o_a=w_o_a.astype(BF16),
        mla=[_prep_mla_weights(w_dqkv_b[j], q_norm_b[j], kv_norm_b[j], w_uq_b[j], w_ukv_b[j])
             for j in range(w_dqkv_b.shape[0])],
        w_o_b=w_o_b.astype(BF16),
        w_up=w_up.astype(BF16),
        w_down=w_down.astype(BF16),
        ln_g=ln_g.astype(F32),
        ln_b=ln_b.astype(F32),
    )
    return (_trunk(x_prompt, weights), _trunk(x_sample, weights))
```

```python
import functools
import math

import jax
import jax.numpy as jnp
from jax import lax
from jax.experimental import pallas as pl
from jax.experimental.pallas import tpu as pltpu

D_MODEL = 2048
DEPTH = 4
N_MIXERS = 2
BLK = 128
A_HEADS = 16
A_KV_HEADS = 4
A_GROUP = A_HEADS // A_KV_HEADS
A_HEAD_DIM = D_MODEL // A_HEADS
WINDOW = 128
B_HEADS = 16
Q_LORA = 448
KV_LORA = 128
NOPE_DIM = 128
ROPE_DIM = 64
V_DIM = 128
D_FF = 4 * D_MODEL
ROPE_THETA = 10000.0
LN_EPS = 1e-5
RMS_EPS = 1e-6
ALPHA = (2 * DEPTH) ** 0.25

LANES = 128
MXU_DIM = 256
VMEM_LIMIT_BYTES = 56 * 1024 * 1024

Q_LORA_PAD = 512
LAT_COLS = Q_LORA_PAD + KV_LORA + 2 * ROPE_DIM
QK_SLAB = MXU_DIM
BF16_SUBLANES = 16
VT_SLAB = V_DIM + BF16_SUBLANES
assert A_HEAD_DIM == V_DIM
MLA_SCORE_SCRATCH_BYTES = 32 * 1024 * 1024

F32 = jnp.float32
BF16 = jnp.bfloat16


def _compiler_params(semantics):
    return pltpu.CompilerParams(dimension_semantics=semantics,
                                vmem_limit_bytes=VMEM_LIMIT_BYTES)


def _layer_norm_rows(y, g, b):
    mu = jnp.mean(y, axis=-1, keepdims=True)
    d = y - mu
    var = jnp.mean(d * d, axis=-1, keepdims=True)
    return d * lax.rsqrt(var + LN_EPS) * g + b


def _dot_nt(a, b):
    return lax.dot_general(a, b, (((1,), (1,)), ((), ())), preferred_element_type=F32)


def _qkv_rope_kernel(x_ref, w_ref, wvt_ref, cos_ref, sin_ref, q_ref, k_ref, vt_ref):
    xb = x_ref[...].astype(BF16)
    cos = cos_ref[...]
    sin = sin_ref[...]
    chunk = A_GROUP * A_HEAD_DIM
    q_cols = A_HEADS * A_HEAD_DIM
    k_cols = A_KV_HEADS * A_HEAD_DIM

    def rope(t):
        return t * cos + pltpu.roll(t, A_HEAD_DIM // 2, 1) * sin

    for c in range(0, q_cols + k_cols, chunk):
        acc = jnp.dot(xb, w_ref[:, c:c + chunk], preferred_element_type=F32)
        for hh in range(chunk // A_HEAD_DIM):
            y = rope(acc[:, hh * A_HEAD_DIM:(hh + 1) * A_HEAD_DIM]).astype(BF16)
            col = c + hh * A_HEAD_DIM
            if col < q_cols:
                q_ref[:, col:col + A_HEAD_DIM] = y
            else:
                k_ref[:, col - q_cols:col - q_cols + A_HEAD_DIM] = y
    vt = _dot_nt(wvt_ref[...], xb).astype(BF16)
    ones = jnp.ones((VT_SLAB - V_DIM, xb.shape[0]), BF16)
    for h in range(A_KV_HEADS):
        vt_ref[h * VT_SLAB:h * VT_SLAB + A_HEAD_DIM, :] = vt[h * A_HEAD_DIM:(h + 1) * A_HEAD_DIM]
        vt_ref[h * VT_SLAB + A_HEAD_DIM:(h + 1) * VT_SLAB, :] = ones


def _qkv_rope(x, w_stack, wvt_stack, layer, cos, sin, seq, tm):
    t = x.shape[0]
    tiles_per_seq = seq // tm
    q_cols = A_HEADS * A_HEAD_DIM
    k_cols = A_KV_HEADS * A_HEAD_DIM
    row = lambda i: (i, 0)
    pos = lambda i: (i % tiles_per_seq, 0)
    return pl.pallas_call(
        _qkv_rope_kernel,
        grid=(t // tm,),
        in_specs=[
            pl.BlockSpec((tm, D_MODEL), row),
            pl.BlockSpec((None, D_MODEL, q_cols + k_cols), lambda i: (layer, 0, 0)),
            pl.BlockSpec((None, k_cols, D_MODEL), lambda i: (layer, 0, 0)),
            pl.BlockSpec((tm, A_HEAD_DIM), pos),
            pl.BlockSpec((tm, A_HEAD_DIM), pos),
        ],
        out_specs=[
            pl.BlockSpec((tm, q_cols), row),
            pl.BlockSpec((tm, k_cols), row),
            pl.BlockSpec((A_KV_HEADS * VT_SLAB, tm), lambda i: (0, i)),
        ],
        out_shape=[
            jax.ShapeDtypeStruct((t, q_cols), BF16),
            jax.ShapeDtypeStruct((t, k_cols), BF16),
            jax.ShapeDtypeStruct((A_KV_HEADS * VT_SLAB, t), BF16),
        ],
        compiler_params=_compiler_params(("parallel",)),
        name="qkv_rope",
    )(x, w_stack, wvt_stack, cos, sin)


def _win_attn_kernel(sink_ref, q_ref, kp_ref, kc_ref, kn_ref, vp_ref, vc_ref, vn_ref,
                     o_ref, *, nb):
    n = pl.program_id(1)
    cols = A_GROUP * BLK
    span = BLK + 2 * WINDOW
    key = lax.broadcasted_iota(jnp.int32, (span, cols), 0)
    qi = lax.broadcasted_iota(jnp.int32, (span, cols), 1) & (BLK - 1)
    lo = jnp.where(n > 0, 0, WINDOW)
    hi = jnp.where(n < nb - 1, span, BLK + WINDOW)
    valid = (key >= jnp.maximum(qi, lo)) & (key <= qi + 2 * WINDOW) & (key < hi)
    bias = jnp.where(valid, 0.0, -jnp.inf).astype(F32)
    grp = lax.broadcasted_iota(jnp.int32, (1, cols), 1) // BLK
    log2e = math.log2(math.e)
    scale2 = log2e / math.sqrt(A_HEAD_DIM)

    scores = []
    for h in range(A_KV_HEADS):
        ks = slice(h * A_HEAD_DIM, (h + 1) * A_HEAD_DIM)
        qh = jnp.concatenate(
            [q_ref[:, (h * A_GROUP + g) * A_HEAD_DIM:(h * A_GROUP + g + 1) * A_HEAD_DIM]
             for g in range(A_GROUP)], axis=0)
        kh = jnp.concatenate([kp_ref[:, ks], kc_ref[:, ks], kn_ref[:, ks]], axis=0)
        scores.append(_dot_nt(kh, qh) * scale2 + bias)

    for h in range(A_KV_HEADS):
        vs = slice(h * VT_SLAB, (h + 1) * VT_SLAB)
        vth = jnp.concatenate([vp_ref[vs, :], vc_ref[vs, :], vn_ref[vs, :]], axis=1)
        s = scores[h]
        sink = jnp.full((1, cols), sink_ref[h * A_GROUP] * log2e, F32)
        for g in range(1, A_GROUP):
            sink = jnp.where(grp == g, sink_ref[h * A_GROUP + g] * log2e, sink)
        m = jnp.maximum(jnp.max(s, axis=0, keepdims=True), sink)
        p = jnp.exp2((s - m).astype(BF16))
        acc = jnp.dot(vth, p, preferred_element_type=F32)
        denom = acc[A_HEAD_DIM:A_HEAD_DIM + 1, :] + jnp.exp2(sink - m)
        o = (acc[:A_HEAD_DIM, :] * (1.0 / denom)).T
        for g in range(A_GROUP):
            c0 = (h * A_GROUP + g) * A_HEAD_DIM
            o_ref[:, c0:c0 + A_HEAD_DIM] = o[g * BLK:(g + 1) * BLK].astype(BF16)


def _win_attn(q, k, vt, sink, batch, seq):
    t = q.shape[0]
    nb = seq // BLK
    q_cols = A_HEADS * A_HEAD_DIM
    k_cols = A_KV_HEADS * A_HEAD_DIM
    cur = lambda b, n: b * nb + n
    prev = lambda b, n: b * nb + jnp.maximum(n - 1, 0)
    nxt = lambda b, n: b * nb + jnp.minimum(n + 1, nb - 1)
    k_spec = lambda blk: pl.BlockSpec((BLK, k_cols), lambda b, n: (blk(b, n), 0))
    vt_spec = lambda blk: pl.BlockSpec((A_KV_HEADS * VT_SLAB, BLK), lambda b, n: (0, blk(b, n)))
    return pl.pallas_call(
        functools.partial(_win_attn_kernel, nb=nb),
        grid=(batch, nb),
        in_specs=[
            pl.BlockSpec(memory_space=pltpu.SMEM),
            pl.BlockSpec((BLK, q_cols), lambda b, n: (cur(b, n), 0)),
            k_spec(prev), k_spec(cur), k_spec(nxt),
            vt_spec(prev), vt_spec(cur), vt_spec(nxt),
        ],
        out_specs=pl.BlockSpec((BLK, q_cols), lambda b, n: (cur(b, n), 0)),
        out_shape=jax.ShapeDtypeStruct((t, q_cols), BF16),
        compiler_params=_compiler_params(("parallel", "parallel")),
        name="win_attn",
    )(sink, q, k, k, k, vt, vt, vt)


def _proj_res_ln_kernel(a_ref, w_ref, x_ref, g_ref, b_ref, o_ref):
    h = jnp.dot(a_ref[...], w_ref[...], preferred_element_type=F32)
    o_ref[...] = _layer_norm_rows(ALPHA * x_ref[...] + h, g_ref[...], b_ref[...])


def _proj_res_ln(a, w_stack, layer, x, g, b, tm):
    t, kdim = a.shape
    row = lambda i: (i, 0)
    const = lambda i: (0, 0)
    return pl.pallas_call(
        _proj_res_ln_kernel,
        grid=(t // tm,),
        in_specs=[
            pl.BlockSpec((tm, kdim), row),
            pl.BlockSpec((None, kdim, D_MODEL), lambda i: (layer, 0, 0)),
            pl.BlockSpec((tm, D_MODEL), row),
            pl.BlockSpec((1, D_MODEL), const),
            pl.BlockSpec((1, D_MODEL), const),
        ],
        out_specs=pl.BlockSpec((tm, D_MODEL), row),
        out_shape=jax.ShapeDtypeStruct((t, D_MODEL), F32),
        compiler_params=_compiler_params(("parallel",)),
        name="proj_res_ln",
    )(a, w_stack, x, g, b)


def _mlp_kernel(x_ref, wu_ref, wd_ref, g_ref, b_ref, o_ref, xb_ref):
    j = pl.program_id(1)

    @pl.when(j == 0)
    def _():
        xb_ref[...] = x_ref[...].astype(BF16)
        o_ref[...] = jnp.zeros_like(o_ref)

    h = jnp.dot(xb_ref[...], wu_ref[...], preferred_element_type=F32)
    h = jnp.square(jnp.maximum(h, 0.0)).astype(BF16)
    o_ref[...] += jnp.dot(h, wd_ref[...], preferred_element_type=F32)

    @pl.when(j == pl.num_programs(1) - 1)
    def _():
        o_ref[...] = _layer_norm_rows(ALPHA * x_ref[...] + o_ref[...], g_ref[...], b_ref[...])


def _mlp(x, wu_stack, wd_stack, layer, g, b, tm, tf):
    t = x.shape[0]
    row = lambda i, j: (i, 0)
    const = lambda i, j: (0, 0)
    return pl.pallas_call(
        _mlp_kernel,
        grid=(t // tm, D_FF // tf),
        in_specs=[
            pl.BlockSpec((tm, D_MODEL), row),
            pl.BlockSpec((None, D_MODEL, tf), lambda i, j: (layer, 0, j)),
            pl.BlockSpec((None, tf, D_MODEL), lambda i, j: (layer, j, 0)),
            pl.BlockSpec((1, D_MODEL), const),
            pl.BlockSpec((1, D_MODEL), const),
        ],
        out_specs=pl.BlockSpec((tm, D_MODEL), row),
        out_shape=jax.ShapeDtypeStruct((t, D_MODEL), F32),
        scratch_shapes=[pltpu.VMEM((tm, D_MODEL), BF16)],
        compiler_params=_compiler_params(("parallel", "arbitrary")),
        name="mlp",
    )(x, wu_stack, wd_stack, g, b)


def _mla_proj_kernel(x_ref, wd_ref, gq_ref, gkv_ref, wuq_ref, wuk_ref, wuvt_ref, tab_ref,
                     q_ref, k_ref, vt_ref):
    xb = x_ref[...].astype(BF16)
    tab = tab_ref[...]
    lane = lax.broadcasted_iota(jnp.int32, tab.shape, 1)

    def rope_pair(y):
        t = y * tab
        return jnp.where(lane < ROPE_DIM, t + pltpu.roll(t, ROPE_DIM, 1), 0.0).astype(BF16)

    lat = jnp.dot(xb, wd_ref[...], preferred_element_type=F32)
    cq = lat[:, :Q_LORA_PAD]
    cq = cq * lax.rsqrt(jnp.sum(cq * cq, axis=-1, keepdims=True) * (1.0 / Q_LORA) + RMS_EPS)
    cq = (cq * gq_ref[...]).astype(BF16)
    ckv = lat[:, Q_LORA_PAD:Q_LORA_PAD + KV_LORA]
    ckv = ckv * lax.rsqrt(jnp.mean(ckv * ckv, axis=-1, keepdims=True) + RMS_EPS)
    ckv = (ckv * gkv_ref[...]).astype(BF16)
    kr = rope_pair(lat[:, Q_LORA_PAD + KV_LORA:])

    heads_per_chunk = 2
    qchunk = heads_per_chunk * QK_SLAB
    for c in range(0, B_HEADS * QK_SLAB, qchunk):
        qq = jnp.dot(cq, wuq_ref[:, c:c + qchunk], preferred_element_type=F32)
        for hh in range(heads_per_chunk):
            o = hh * QK_SLAB
            q_ref[:, c + o:c + o + NOPE_DIM] = qq[:, o:o + NOPE_DIM].astype(BF16)
            q_ref[:, c + o + NOPE_DIM:c + o + QK_SLAB] = rope_pair(qq[:, o + NOPE_DIM:o + QK_SLAB])

    kchunk = 4 * NOPE_DIM
    for c in range(0, B_HEADS * NOPE_DIM, kchunk):
        kn = jnp.dot(ckv, wuk_ref[:, c:c + kchunk], preferred_element_type=F32)
        for hh in range(kchunk // NOPE_DIM):
            h = c // NOPE_DIM + hh
            k_ref[:, h * QK_SLAB:h * QK_SLAB + NOPE_DIM] = (
                kn[:, hh * NOPE_DIM:(hh + 1) * NOPE_DIM].astype(BF16))
            k_ref[:, h * QK_SLAB + NOPE_DIM:(h + 1) * QK_SLAB] = kr
    ones = jnp.ones((VT_SLAB - V_DIM, ckv.shape[0]), BF16)
    for c in range(0, B_HEADS * V_DIM, kchunk):
        vt = _dot_nt(wuvt_ref[c:c + kchunk, :], ckv).astype(BF16)
        for hh in range(kchunk // V_DIM):
            r0 = (c // V_DIM + hh) * VT_SLAB
            vt_ref[r0:r0 + V_DIM, :] = vt[hh * V_DIM:(hh + 1) * V_DIM]
            vt_ref[r0 + V_DIM:r0 + VT_SLAB, :] = ones


def _mla_proj(x, wd, gq, gkv, wuq, wuk, wuvt, tab, seq, tm):
    t = x.shape[0]
    tiles_per_seq = seq // tm
    row = lambda i: (i, 0)
    const = lambda i: (0, 0)
    full = lambda a: pl.BlockSpec(a.shape, const)
    return pl.pallas_call(
        _mla_proj_kernel,
        grid=(t // tm,),
        in_specs=[
            pl.BlockSpec((tm, D_MODEL), row),
            full(wd), full(gq), full(gkv), full(wuq), full(wuk), full(wuvt),
            pl.BlockSpec((tm, LANES), lambda i: (i % tiles_per_seq, 0)),
        ],
        out_specs=[
            pl.BlockSpec((tm, B_HEADS * QK_SLAB), row),
            pl.BlockSpec((tm, B_HEADS * QK_SLAB), row),
            pl.BlockSpec((B_HEADS * VT_SLAB, tm), lambda i: (0, i)),
        ],
        out_shape=[
            jax.ShapeDtypeStruct((t, B_HEADS * QK_SLAB), BF16),
            jax.ShapeDtypeStruct((t, B_HEADS * QK_SLAB), BF16),
            jax.ShapeDtypeStruct((B_HEADS * VT_SLAB, t), BF16),
        ],
        compiler_params=_compiler_params(("parallel",)),
        name="mla_proj",
    )(x, wd, gq, gkv, wuq, wuk, wuvt, tab)


F32_SUBLANES = 8
MLA_EXP2_SCALE = math.log2(math.e) / math.sqrt(NOPE_DIM + ROPE_DIM)


def _mla_attn_kernel(q_ref, k_ref, vt_ref, o_ref, s_ref, acc_ref, *, tk1, tk2, unroll):
    seq = k_ref.shape[0]
    tq = q_ref.shape[0]
    sub = F32_SUBLANES
    q = q_ref[...]

    m8 = jnp.full((sub, tq), -jnp.inf, F32)
    for c0 in range(0, seq, tk1):
        s = _dot_nt(k_ref[c0:c0 + tk1, :], q) * MLA_EXP2_SCALE
        s_ref[c0:c0 + tk1, :] = s
        m8 = jnp.maximum(m8, jnp.max(s.reshape(tk1 // sub, sub, tq), axis=0))
    m = jnp.max(m8, axis=0, keepdims=True)

    acc_ref[...] = jnp.zeros_like(acc_ref)

    def body(c, carry):
        start = pl.multiple_of(c * tk2, tk2)
        p = jnp.exp2((s_ref[pl.ds(start, tk2), :] - m).astype(BF16))
        acc_ref[...] += jnp.dot(vt_ref[:, pl.ds(start, tk2)], p, preferred_element_type=F32)
        return carry

    lax.fori_loop(0, seq // tk2, body, 0, unroll=unroll)
    inv_l = 1.0 / acc_ref[V_DIM:V_DIM + 1, :]
    o_ref[...] = (acc_ref[:V_DIM, :] * inv_l).T.astype(BF16)


def _mla_attn(q, k, vt, batch, seq, tq, tk1, tk2, unroll):
    t = q.shape[0]
    nq = seq // tq
    return pl.pallas_call(
        functools.partial(_mla_attn_kernel, tk1=tk1, tk2=tk2, unroll=unroll),
        grid=(batch, B_HEADS, nq),
        in_specs=[
            pl.BlockSpec((tq, QK_SLAB), lambda b, h, i: (b * nq + i, h)),
            pl.BlockSpec((seq, QK_SLAB), lambda b, h, i: (b, h)),
            pl.BlockSpec((VT_SLAB, seq), lambda b, h, i: (h, b)),
        ],
        out_specs=pl.BlockSpec((tq, V_DIM), lambda b, h, i: (b * nq + i, h)),
        out_shape=jax.ShapeDtypeStruct((t, B_HEADS * V_DIM), BF16),
        scratch_shapes=[
            pltpu.VMEM((seq, tq), F32),
            pltpu.VMEM((VT_SLAB, tq), F32),
        ],
        compiler_params=_compiler_params(("parallel", "parallel", "arbitrary")),
        name="mla_attn",
    )(q, k, vt)


def _rope_angles(seq, dim):
    inv = 1.0 / (ROPE_THETA ** (jnp.arange(0, dim, 2, dtype=F32) / dim))
    ang = jnp.arange(seq, dtype=F32)[:, None] * inv[None, :]
    return jnp.concatenate([ang, ang], axis=-1)


def _rot_half_cols(w):
    half = w.shape[-1] // 2
    return jnp.concatenate([-w[:, half:], w[:, :half]], axis=-1)


def _prep_mla_weights(w_dqkv, q_norm, kv_norm, w_uq, w_ukv):
    wq = w_dqkv[:, :Q_LORA]
    wkv = w_dqkv[:, Q_LORA:Q_LORA + KV_LORA]
    wr = w_dqkv[:, Q_LORA + KV_LORA:]
    zpad = jnp.zeros((D_MODEL, Q_LORA_PAD - Q_LORA), w_dqkv.dtype)
    wd = jnp.concatenate([wq, zpad, wkv, wr, _rot_half_cols(wr)], axis=1).astype(BF16)
    gq = jnp.pad(q_norm, (0, Q_LORA_PAD - Q_LORA)).reshape(1, Q_LORA_PAD)
    gkv = kv_norm.reshape(1, KV_LORA)
    uq = w_uq.reshape(Q_LORA, B_HEADS, NOPE_DIM + ROPE_DIM)
    uq_r = uq[:, :, NOPE_DIM:]
    uq_rot = jnp.concatenate([-uq_r[..., ROPE_DIM // 2:], uq_r[..., :ROPE_DIM // 2]], axis=-1)
    wuq = jnp.concatenate([uq, uq_rot], axis=-1).reshape(Q_LORA, B_HEADS * QK_SLAB)
    wuq = jnp.pad(wuq, ((0, Q_LORA_PAD - Q_LORA), (0, 0))).astype(BF16)
    ukv = w_ukv.reshape(KV_LORA, B_HEADS, NOPE_DIM + V_DIM)
    wuk = ukv[:, :, :NOPE_DIM].reshape(KV_LORA, B_HEADS * NOPE_DIM).astype(BF16)
    wuvt = ukv[:, :, NOPE_DIM:].reshape(KV_LORA, B_HEADS * V_DIM).T.astype(BF16)
    return wd, gq, gkv, wuq, wuk, wuvt


def _tiles(seq):
    tm = 512
    assert seq % tm == 0 and seq % BLK == 0
    tk2 = 512
    unroll = max(1, min(8, seq // tk2 // 2))
    tq = min(seq, MLA_SCORE_SCRATCH_BYTES // (4 * seq))
    assert seq % tq == 0 and tq % MXU_DIM == 0
    return dict(tm=tm, tm_mlp=1024, tf=512, tq=tq, tk1=min(seq, 2048), tk2=tk2,
                unroll=unroll)


def _trunk(x3, weights):
    batch, seq, _ = x3.shape
    cfg = _tiles(seq)
    tm = cfg["tm"]
    x = x3.reshape(batch * seq, D_MODEL)

    ang_a = _rope_angles(seq, A_HEAD_DIM)
    sign = jnp.where(jnp.arange(A_HEAD_DIM) < A_HEAD_DIM // 2, -1.0, 1.0).astype(F32)
    cos_a, sin_a = jnp.cos(ang_a), jnp.sin(ang_a) * sign
    ang_b = _rope_angles(seq, ROPE_DIM)
    tab_b = jnp.concatenate([jnp.cos(ang_b), jnp.sin(ang_b)], axis=-1)

    for i in range(DEPTH):
        j = i // N_MIXERS
        if i % N_MIXERS == 0:
            q, k, v = _qkv_rope(x, weights["w_qkv_a"], weights["w_vt_a"], j, cos_a, sin_a,
                                seq, tm)
            attn = _win_attn(q, k, v, weights["sink_a"][j], batch, seq)
            w_o = weights["w_o_a"]
        else:
            q, k, v = _mla_proj(x, *weights["mla"][j], tab_b, seq, tm)
            attn = _mla_attn(q, k, v, batch, seq, cfg["tq"], cfg["tk1"], cfg["tk2"],
                             cfg["unroll"])
            w_o = weights["w_o_b"]
        g, b = weights["ln_g"], weights["ln_b"]
        x = _proj_res_ln(attn, w_o, j, x, g[i, 0:1], b[i, 0:1], tm)
        x = _mlp(x, weights["w_up"], weights["w_down"], i, g[i, 1:2], b[i, 1:2],
                 cfg["tm_mlp"], cfg["tf"])
    return x.reshape(batch, seq, D_MODEL)


def kernel(x_prompt, x_sample, w_qkv_a, sink_a, w_o_a, w_dqkv_b, q_norm_b, kv_norm_b,
           w_uq_b, w_ukv_b, w_o_b, w_up, w_down, ln_g, ln_b):
    weights = dict(
        w_qkv_a=w_qkv_a.astype(BF16),
        w_vt_a=jnp.swapaxes(w_qkv_a[:, :, (A_HEADS + A_KV_HEADS) * A_HEAD_DIM:], 1, 2).astype(BF16),
        sink_a=sink_a.astype(F32),
        w_o_a=w_o_a.astype(BF16),
        mla=[_prep_mla_weights(w_dqkv_b[j], q_norm_b[j], kv_norm_b[j], w_uq_b[j], w_ukv_b[j])
             for j in range(w_dqkv_b.shape[0])],
        w_o_b=w_o_b.astype(BF16),
        w_up=w_up.astype(BF16),
        w_down=w_down.astype(BF16),
        ln_g=ln_g.astype(F32),
        ln_b=ln_b.astype(F32),
    )
    return (_trunk(x_prompt, weights), _trunk(x_sample, weights))
```

```python
import functools
import math

import jax
import jax.numpy as jnp
from jax import lax
from jax.experimental import pallas as pl
from jax.experimental.pallas import tpu as pltpu

D_MODEL = 2048
DEPTH = 4
N_MIXERS = 2
BLK = 128
A_HEADS = 16
A_KV_HEADS = 4
A_GROUP = A_HEADS // A_KV_HEADS
A_HEAD_DIM = D_MODEL // A_HEADS
WINDOW = 128
B_HEADS = 16
Q_LORA = 448
KV_LORA = 128
NOPE_DIM = 128
ROPE_DIM = 64
V_DIM = 128
D_FF = 4 * D_MODEL
ROPE_THETA = 10000.0
LN_EPS = 1e-5
RMS_EPS = 1e-6
ALPHA = (2 * DEPTH) ** 0.25

LANES = 128
MXU_DIM = 256
VMEM_LIMIT_BYTES = 56 * 1024 * 1024

Q_LORA_PAD = 512
LAT_COLS = Q_LORA_PAD + KV_LORA + 2 * ROPE_DIM
QK_SLAB = MXU_DIM
BF16_SUBLANES = 16
VT_SLAB = V_DIM + BF16_SUBLANES
assert A_HEAD_DIM == V_DIM
MLA_SCORE_SCRATCH_BYTES = 32 * 1024 * 1024

F32 = jnp.float32
BF16 = jnp.bfloat16


RESIDENT = pl.Buffered(1)


def _compiler_params(semantics):
    return pltpu.CompilerParams(dimension_semantics=semantics,
                                vmem_limit_bytes=VMEM_LIMIT_BYTES)


def _layer_norm_rows(y, g, b):
    mu = jnp.mean(y, axis=-1, keepdims=True)
    d = y - mu
    var = jnp.mean(d * d, axis=-1, keepdims=True)
    return d * lax.rsqrt(var + LN_EPS) * g + b


def _dot_nt(a, b):
    return lax.dot_general(a, b, (((1,), (1,)), ((), ())), preferred_element_type=F32)


def _qkv_rope_kernel(x_ref, w_ref, wvt_ref, cos_ref, sin_ref, q_ref, k_ref, vt_ref):
    xb = x_ref[...].astype(BF16)
    cos = cos_ref[...]
    sin = sin_ref[...]
    chunk = A_GROUP * A_HEAD_DIM
    q_cols = A_HEADS * A_HEAD_DIM
    k_cols = A_KV_HEADS * A_HEAD_DIM

    def rope(t):
        return t * cos + pltpu.roll(t, A_HEAD_DIM // 2, 1) * sin

    for c in range(0, q_cols + k_cols, chunk):
        acc = jnp.dot(xb, w_ref[:, c:c + chunk], preferred_element_type=F32)
        for hh in range(chunk // A_HEAD_DIM):
            y = rope(acc[:, hh * A_HEAD_DIM:(hh + 1) * A_HEAD_DIM]).astype(BF16)
            col = c + hh * A_HEAD_DIM
            if col < q_cols:
                q_ref[:, col:col + A_HEAD_DIM] = y
            else:
                k_ref[:, col - q_cols:col - q_cols + A_HEAD_DIM] = y
    vt = _dot_nt(wvt_ref[...], xb).astype(BF16)
    ones = jnp.ones((VT_SLAB - V_DIM, xb.shape[0]), BF16)
    for h in range(A_KV_HEADS):
        vt_ref[h * VT_SLAB:h * VT_SLAB + A_HEAD_DIM, :] = vt[h * A_HEAD_DIM:(h + 1) * A_HEAD_DIM]
        vt_ref[h * VT_SLAB + A_HEAD_DIM:(h + 1) * VT_SLAB, :] = ones


def _qkv_rope(x, w_stack, wvt_stack, layer, cos, sin, seq, tm):
    t = x.shape[0]
    tiles_per_seq = seq // tm
    q_cols = A_HEADS * A_HEAD_DIM
    k_cols = A_KV_HEADS * A_HEAD_DIM
    row = lambda i: (i, 0)
    pos = lambda i: (i % tiles_per_seq, 0)
    return pl.pallas_call(
        _qkv_rope_kernel,
        grid=(t // tm,),
        in_specs=[
            pl.BlockSpec((tm, D_MODEL), row),
            pl.BlockSpec((None, D_MODEL, q_cols + k_cols), lambda i: (layer, 0, 0),
                         pipeline_mode=RESIDENT),
            pl.BlockSpec((None, k_cols, D_MODEL), lambda i: (layer, 0, 0),
                         pipeline_mode=RESIDENT),
            pl.BlockSpec((tm, A_HEAD_DIM), pos),
            pl.BlockSpec((tm, A_HEAD_DIM), pos),
        ],
        out_specs=[
            pl.BlockSpec((tm, q_cols), row),
            pl.BlockSpec((tm, k_cols), row),
            pl.BlockSpec((A_KV_HEADS * VT_SLAB, tm), lambda i: (0, i)),
        ],
        out_shape=[
            jax.ShapeDtypeStruct((t, q_cols), BF16),
            jax.ShapeDtypeStruct((t, k_cols), BF16),
            jax.ShapeDtypeStruct((A_KV_HEADS * VT_SLAB, t), BF16),
        ],
        compiler_params=_compiler_params(("parallel",)),
        name="qkv_rope",
    )(x, w_stack, wvt_stack, cos, sin)


def _win_attn_kernel(sink_ref, q_ref, kp_ref, kc_ref, kn_ref, vp_ref, vc_ref, vn_ref,
                     o_ref, *, nb):
    n = pl.program_id(1)
    cols = A_GROUP * BLK
    span = BLK + 2 * WINDOW
    key = lax.broadcasted_iota(jnp.int32, (span, cols), 0)
    qi = lax.broadcasted_iota(jnp.int32, (span, cols), 1) & (BLK - 1)
    lo = jnp.where(n > 0, 0, WINDOW)
    hi = jnp.where(n < nb - 1, span, BLK + WINDOW)
    valid = (key >= jnp.maximum(qi, lo)) & (key <= qi + 2 * WINDOW) & (key < hi)
    bias = jnp.where(valid, 0.0, -jnp.inf).astype(F32)
    grp = lax.broadcasted_iota(jnp.int32, (1, cols), 1) // BLK
    log2e = math.log2(math.e)
    scale2 = log2e / math.sqrt(A_HEAD_DIM)

    scores = []
    for h in range(A_KV_HEADS):
        ks = slice(h * A_HEAD_DIM, (h + 1) * A_HEAD_DIM)
        qh = jnp.concatenate(
            [q_ref[:, (h * A_GROUP + g) * A_HEAD_DIM:(h * A_GROUP + g + 1) * A_HEAD_DIM]
             for g in range(A_GROUP)], axis=0)
        kh = jnp.concatenate([kp_ref[:, ks], kc_ref[:, ks], kn_ref[:, ks]], axis=0)
        scores.append(_dot_nt(kh, qh) * scale2 + bias)

    for h in range(A_KV_HEADS):
        vs = slice(h * VT_SLAB, (h + 1) * VT_SLAB)
        vth = jnp.concatenate([vp_ref[vs, :], vc_ref[vs, :], vn_ref[vs, :]], axis=1)
        s = scores[h]
        sink = jnp.full((1, cols), sink_ref[h * A_GROUP] * log2e, F32)
        for g in range(1, A_GROUP):
            sink = jnp.where(grp == g, sink_ref[h * A_GROUP + g] * log2e, sink)
        m = jnp.maximum(jnp.max(s, axis=0, keepdims=True), sink)
        p = jnp.exp2((s - m).astype(BF16))
        acc = jnp.dot(vth, p, preferred_element_type=F32)
        denom = acc[A_HEAD_DIM:A_HEAD_DIM + 1, :] + jnp.exp2(sink - m)
        o = (acc[:A_HEAD_DIM, :] * (1.0 / denom)).T
        for g in range(A_GROUP):
            c0 = (h * A_GROUP + g) * A_HEAD_DIM
            o_ref[:, c0:c0 + A_HEAD_DIM] = o[g * BLK:(g + 1) * BLK].astype(BF16)


def _win_attn(q, k, vt, sink, batch, seq):
    t = q.shape[0]
    nb = seq // BLK
    q_cols = A_HEADS * A_HEAD_DIM
    k_cols = A_KV_HEADS * A_HEAD_DIM
    cur = lambda b, n: b * nb + n
    prev = lambda b, n: b * nb + jnp.maximum(n - 1, 0)
    nxt = lambda b, n: b * nb + jnp.minimum(n + 1, nb - 1)
    k_spec = lambda blk: pl.BlockSpec((BLK, k_cols), lambda b, n: (blk(b, n), 0))
    vt_spec = lambda blk: pl.BlockSpec((A_KV_HEADS * VT_SLAB, BLK), lambda b, n: (0, blk(b, n)))
    return pl.pallas_call(
        functools.partial(_win_attn_kernel, nb=nb),
        grid=(batch, nb),
        in_specs=[
            pl.BlockSpec(memory_space=pltpu.SMEM),
            pl.BlockSpec((BLK, q_cols), lambda b, n: (cur(b, n), 0)),
            k_spec(prev), k_spec(cur), k_spec(nxt),
            vt_spec(prev), vt_spec(cur), vt_spec(nxt),
        ],
        out_specs=pl.BlockSpec((BLK, q_cols), lambda b, n: (cur(b, n), 0)),
        out_shape=jax.ShapeDtypeStruct((t, q_cols), BF16),
        compiler_params=_compiler_params(("parallel", "parallel")),
        name="win_attn",
    )(sink, q, k, k, k, vt, vt, vt)


def _proj_res_ln_kernel(a_ref, w_ref, x_ref, g_ref, b_ref, o_ref):
    h = jnp.dot(a_ref[...], w_ref[...], preferred_element_type=F32)
    o_ref[...] = _layer_norm_rows(ALPHA * x_ref[...] + h, g_ref[...], b_ref[...])


def _proj_res_ln(a, w_stack, layer, x, g, b, tm):
    t, kdim = a.shape
    row = lambda i: (i, 0)
    const = lambda i: (0, 0)
    return pl.pallas_call(
        _proj_res_ln_kernel,
        grid=(t // tm,),
        in_specs=[
            pl.BlockSpec((tm, kdim), row),
            pl.BlockSpec((None, kdim, D_MODEL), lambda i: (layer, 0, 0), pipeline_mode=RESIDENT),
            pl.BlockSpec((tm, D_MODEL), row),
            pl.BlockSpec((1, D_MODEL), const),
            pl.BlockSpec((1, D_MODEL), const),
        ],
        out_specs=pl.BlockSpec((tm, D_MODEL), row),
        out_shape=jax.ShapeDtypeStruct((t, D_MODEL), F32),
        compiler_params=_compiler_params(("parallel",)),
        name="proj_res_ln",
    )(a, w_stack, x, g, b)


def _mlp_kernel(x_ref, wu_ref, wd_ref, g_ref, b_ref, o_ref, xb_ref):
    j = pl.program_id(1)

    @pl.when(j == 0)
    def _():
        xb_ref[...] = x_ref[...].astype(BF16)
        o_ref[...] = jnp.zeros_like(o_ref)

    h = jnp.dot(xb_ref[...], wu_ref[...], preferred_element_type=F32)
    h = jnp.square(jnp.maximum(h, 0.0)).astype(BF16)
    o_ref[...] += jnp.dot(h, wd_ref[...], preferred_element_type=F32)

    @pl.when(j == pl.num_programs(1) - 1)
    def _():
        o_ref[...] = _layer_norm_rows(ALPHA * x_ref[...] + o_ref[...], g_ref[...], b_ref[...])


def _mlp(x, wu_stack, wd_stack, layer, g, b, tm, tf):
    t = x.shape[0]
    row = lambda i, j: (i, 0)
    const = lambda i, j: (0, 0)
    return pl.pallas_call(
        _mlp_kernel,
        grid=(t // tm, D_FF // tf),
        in_specs=[
            pl.BlockSpec((tm, D_MODEL), row),
            pl.BlockSpec((None, D_MODEL, tf), lambda i, j: (layer, 0, j)),
            pl.BlockSpec((None, tf, D_MODEL), lambda i, j: (layer, j, 0)),
            pl.BlockSpec((1, D_MODEL), const),
            pl.BlockSpec((1, D_MODEL), const),
        ],
        out_specs=pl.BlockSpec((tm, D_MODEL), row),
        out_shape=jax.ShapeDtypeStruct((t, D_MODEL), F32),
        scratch_shapes=[pltpu.VMEM((tm, D_MODEL), BF16)],
        compiler_params=_compiler_params(("parallel", "arbitrary")),
        name="mlp",
    )(x, wu_stack, wd_stack, g, b)


def _mla_proj_kernel(x_ref, wd_ref, gq_ref, gkv_ref, wuq_ref, wuk_ref, wuvt_ref, tab_ref,
                     q_ref, k_ref, vt_ref):
    xb = x_ref[...].astype(BF16)
    tab = tab_ref[...]
    lane = lax.broadcasted_iota(jnp.int32, tab.shape, 1)

    def rope_pair(y):
        t = y * tab
        return jnp.where(lane < ROPE_DIM, t + pltpu.roll(t, ROPE_DIM, 1), 0.0).astype(BF16)

    lat = jnp.dot(xb, wd_ref[...], preferred_element_type=F32)
    cq = lat[:, :Q_LORA_PAD]
    cq = cq * lax.rsqrt(jnp.sum(cq * cq, axis=-1, keepdims=True) * (1.0 / Q_LORA) + RMS_EPS)
    cq = (cq * gq_ref[...]).astype(BF16)
    ckv = lat[:, Q_LORA_PAD:Q_LORA_PAD + KV_LORA]
    ckv = ckv * lax.rsqrt(jnp.mean(ckv * ckv, axis=-1, keepdims=True) + RMS_EPS)
    ckv = (ckv * gkv_ref[...]).astype(BF16)
    kr = rope_pair(lat[:, Q_LORA_PAD + KV_LORA:])

    heads_per_chunk = 2
    qchunk = heads_per_chunk * QK_SLAB
    for c in range(0, B_HEADS * QK_SLAB, qchunk):
        qq = jnp.dot(cq, wuq_ref[:, c:c + qchunk], preferred_element_type=F32)
        for hh in range(heads_per_chunk):
            o = hh * QK_SLAB
            q_ref[:, c + o:c + o + NOPE_DIM] = qq[:, o:o + NOPE_DIM].astype(BF16)
            q_ref[:, c + o + NOPE_DIM:c + o + QK_SLAB] = rope_pair(qq[:, o + NOPE_DIM:o + QK_SLAB])

    kchunk = 4 * NOPE_DIM
    for c in range(0, B_HEADS * NOPE_DIM, kchunk):
        kn = jnp.dot(ckv, wuk_ref[:, c:c + kchunk], preferred_element_type=F32)
        for hh in range(kchunk // NOPE_DIM):
            h = c // NOPE_DIM + hh
            k_ref[:, h * QK_SLAB:h * QK_SLAB + NOPE_DIM] = (
                kn[:, hh * NOPE_DIM:(hh + 1) * NOPE_DIM].astype(BF16))
            k_ref[:, h * QK_SLAB + NOPE_DIM:(h + 1) * QK_SLAB] = kr
    ones = jnp.ones((VT_SLAB - V_DIM, ckv.shape[0]), BF16)
    for c in range(0, B_HEADS * V_DIM, kchunk):
        vt = _dot_nt(wuvt_ref[c:c + kchunk, :], ckv).astype(BF16)
        for hh in range(kchunk // V_DIM):
            r0 = (c // V_DIM + hh) * VT_SLAB
            vt_ref[r0:r0 + V_DIM, :] = vt[hh * V_DIM:(hh + 1) * V_DIM]
            vt_ref[r0 + V_DIM:r0 + VT_SLAB, :] = ones


def _mla_proj(x, wd, gq, gkv, wuq, wuk, wuvt, tab, seq, tm):
    t = x.shape[0]
    tiles_per_seq = seq // tm
    row = lambda i: (i, 0)
    const = lambda i: (0, 0)
    full = lambda a: pl.BlockSpec(a.shape, const, pipeline_mode=RESIDENT)
    return pl.pallas_call(
        _mla_proj_kernel,
        grid=(t // tm,),
        in_specs=[
            pl.BlockSpec((tm, D_MODEL), row),
            full(wd), full(gq), full(gkv), full(wuq), full(wuk), full(wuvt),
            pl.BlockSpec((tm, LANES), lambda i: (i % tiles_per_seq, 0)),
        ],
        out_specs=[
            pl.BlockSpec((tm, B_HEADS * QK_SLAB), row),
            pl.BlockSpec((tm, B_HEADS * QK_SLAB), row),
            pl.BlockSpec((B_HEADS * VT_SLAB, tm), lambda i: (0, i)),
        ],
        out_shape=[
            jax.ShapeDtypeStruct((t, B_HEADS * QK_SLAB), BF16),
            jax.ShapeDtypeStruct((t, B_HEADS * QK_SLAB), BF16),
            jax.ShapeDtypeStruct((B_HEADS * VT_SLAB, t), BF16),
        ],
        compiler_params=_compiler_params(("parallel",)),
        name="mla_proj",
    )(x, wd, gq, gkv, wuq, wuk, wuvt, tab)


F32_SUBLANES = 8
MLA_EXP2_SCALE = math.log2(math.e) / math.sqrt(NOPE_DIM + ROPE_DIM)


def _mla_attn_kernel(q_ref, k_ref, vt_ref, o_ref, s_ref, acc_ref, *, tk1, tk2, unroll):
    seq = k_ref.shape[0]
    tq = q_ref.shape[0]
    sub = F32_SUBLANES
    q = q_ref[...]

    m8 = jnp.full((sub, tq), -jnp.inf, F32)
    for c0 in range(0, seq, tk1):
        s = _dot_nt(k_ref[c0:c0 + tk1, :], q) * MLA_EXP2_SCALE
        s_ref[c0:c0 + tk1, :] = s
        m8 = jnp.maximum(m8, jnp.max(s.reshape(tk1 // sub, sub, tq), axis=0))
    m = jnp.max(m8, axis=0, keepdims=True)

    acc_ref[...] = jnp.zeros_like(acc_ref)

    def body(c, carry):
        start = pl.multiple_of(c * tk2, tk2)
        p = jnp.exp2((s_ref[pl.ds(start, tk2), :] - m).astype(BF16))
        acc_ref[...] += jnp.dot(vt_ref[:, pl.ds(start, tk2)], p, preferred_element_type=F32)
        return carry

    lax.fori_loop(0, seq // tk2, body, 0, unroll=unroll)
    inv_l = 1.0 / acc_ref[V_DIM:V_DIM + 1, :]
    o_ref[...] = (acc_ref[:V_DIM, :] * inv_l).T.astype(BF16)


def _mla_attn(q, k, vt, batch, seq, tq, tk1, tk2, unroll):
    t = q.shape[0]
    nq = seq // tq
    return pl.pallas_call(
        functools.partial(_mla_attn_kernel, tk1=tk1, tk2=tk2, unroll=unroll),
        grid=(batch, B_HEADS, nq),
        in_specs=[
            pl.BlockSpec((tq, QK_SLAB), lambda b, h, i: (b * nq + i, h)),
            pl.BlockSpec((seq, QK_SLAB), lambda b, h, i: (b, h)),
            pl.BlockSpec((VT_SLAB, seq), lambda b, h, i: (h, b)),
        ],
        out_specs=pl.BlockSpec((tq, V_DIM), lambda b, h, i: (b * nq + i, h)),
        out_shape=jax.ShapeDtypeStruct((t, B_HEADS * V_DIM), BF16),
        scratch_shapes=[
            pltpu.VMEM((seq, tq), F32),
            pltpu.VMEM((VT_SLAB, tq), F32),
        ],
        compiler_params=_compiler_params(("parallel", "parallel", "arbitrary")),
        name="mla_attn",
    )(q, k, vt)


def _rope_angles(seq, dim):
    inv = 1.0 / (ROPE_THETA ** (jnp.arange(0, dim, 2, dtype=F32) / dim))
    ang = jnp.arange(seq, dtype=F32)[:, None] * inv[None, :]
    return jnp.concatenate([ang, ang], axis=-1)


def _rot_half_cols(w):
    half = w.shape[-1] // 2
    return jnp.concatenate([-w[:, half:], w[:, :half]], axis=-1)


def _prep_mla_weights(w_dqkv, q_norm, kv_norm, w_uq, w_ukv):
    wq = w_dqkv[:, :Q_LORA]
    wkv = w_dqkv[:, Q_LORA:Q_LORA + KV_LORA]
    wr = w_dqkv[:, Q_LORA + KV_LORA:]
    zpad = jnp.zeros((D_MODEL, Q_LORA_PAD - Q_LORA), w_dqkv.dtype)
    wd = jnp.concatenate([wq, zpad, wkv, wr, _rot_half_cols(wr)], axis=1).astype(BF16)
    gq = jnp.pad(q_norm, (0, Q_LORA_PAD - Q_LORA)).reshape(1, Q_LORA_PAD)
    gkv = kv_norm.reshape(1, KV_LORA)
    uq = w_uq.reshape(Q_LORA, B_HEADS, NOPE_DIM + ROPE_DIM)
    uq_r = uq[:, :, NOPE_DIM:]
    uq_rot = jnp.concatenate([-uq_r[..., ROPE_DIM // 2:], uq_r[..., :ROPE_DIM // 2]], axis=-1)
    wuq = jnp.concatenate([uq, uq_rot], axis=-1).reshape(Q_LORA, B_HEADS * QK_SLAB)
    wuq = jnp.pad(wuq, ((0, Q_LORA_PAD - Q_LORA), (0, 0))).astype(BF16)
    ukv = w_ukv.reshape(KV_LORA, B_HEADS, NOPE_DIM + V_DIM)
    wuk = ukv[:, :, :NOPE_DIM].reshape(KV_LORA, B_HEADS * NOPE_DIM).astype(BF16)
    wuvt = ukv[:, :, NOPE_DIM:].reshape(KV_LORA, B_HEADS * V_DIM).T.astype(BF16)
    return wd, gq, gkv, wuq, wuk, wuvt


def _tiles(seq):
    tm = 512
    assert seq % tm == 0 and seq % BLK == 0
    tk2 = 512
    unroll = max(1, min(8, seq // tk2 // 2))
    tq = min(seq, MLA_SCORE_SCRATCH_BYTES // (4 * seq))
    assert seq % tq == 0 and tq % MXU_DIM == 0
    return dict(tm=tm, tm_qkv=2 * tm, tf=1024, tq=tq, tk1=min(seq, 2048), tk2=tk2, unroll=unroll)


def _trunk(x3, weights):
    batch, seq, _ = x3.shape
    cfg = _tiles(seq)
    tm = cfg["tm"]
    x = x3.reshape(batch * seq, D_MODEL)

    ang_a = _rope_angles(seq, A_HEAD_DIM)
    sign = jnp.where(jnp.arange(A_HEAD_DIM) < A_HEAD_DIM // 2, -1.0, 1.0).astype(F32)
    cos_a, sin_a = jnp.cos(ang_a), jnp.sin(ang_a) * sign
    ang_b = _rope_angles(seq, ROPE_DIM)
    tab_b = jnp.concatenate([jnp.cos(ang_b), jnp.sin(ang_b)], axis=-1)

    for i in range(DEPTH):
        j = i // N_MIXERS
        if i % N_MIXERS == 0:
            q, k, v = _qkv_rope(x, weights["w_qkv_a"], weights["w_vt_a"], j, cos_a, sin_a,
                                seq, cfg["tm_qkv"])
            attn = _win_attn(q, k, v, weights["sink_a"][j], batch, seq)
            w_o = weights["w_o_a"]
        else:
            q, k, v = _mla_proj(x, *weights["mla"][j], tab_b, seq, tm)
            attn = _mla_attn(q, k, v, batch, seq, cfg["tq"], cfg["tk1"], cfg["tk2"],
                             cfg["unroll"])
            w_o = weights["w_o_b"]
        g, b = weights["ln_g"], weights["ln_b"]
        x = _proj_res_ln(attn, w_o, j, x, g[i, 0:1], b[i, 0:1], tm)
        x = _mlp(x, weights["w_up"], weights["w_down"], i, g[i, 1:2], b[i, 1:2],
                 tm, cfg["tf"])
    return x.reshape(batch, seq, D_MODEL)


def kernel(x_prompt, x_sample, w_qkv_a, sink_a, w_o_a, w_dqkv_b, q_norm_b, kv_norm_b,
           w_uq_b, w_ukv_b, w_o_b, w_up, w_down, ln_g, ln_b):
    weights = dict(
        w_qkv_a=w_qkv_a.astype(BF16),
        w_vt_a=jnp.swapaxes(w_qkv_a[:, :, (A_HEADS + A_KV_HEADS) * A_HEAD_DIM:], 1, 2).astype(BF16),
        sink_a=sink_a.astype(F32),
        w_o_a=w_o_a.astype(BF16),
        mla=[_prep_mla_weights(w_dqkv_b[j], q_norm_b[j], kv_norm_b[j], w_uq_b[j], w_ukv_b[j])
             for j in range(w_dqkv_b.shape[0])],
        w_o_b=w_o_b.astype(BF16),
        w_up=w_up.astype(BF16),
        w_down=w_down.astype(BF16),
        ln_g=ln_g.astype(F32),
        ln_b=ln_b.astype(F32),
    )
    return (_trunk(x_prompt, weights), _trunk(x_sample, weights))
```
